```python
import functools
import jax, jax.numpy as jnp
from jax import lax
import numpy as np

D_MODEL = 1024
BATCH = 4
SEQ = 8192
DEPTH = 1

GRID_W = 64
CTX_LEN = 256
D_CONV = 1024
CONV_W = 3
D_LRU = 1024
LRU_BLOCKS = 16
LRU_BLOCK_W = D_LRU // LRU_BLOCKS
LRU_CONV_W = 4
LRU_C = 8.0
N_BRANCH = 2
N_EXPERTS = 32
TOP_K = 4
D_EXPERT = 1024
SWIGLU_LIMIT = 7.0
SWIGLU_ALPHA = 1.702
MOE_BLOCK = 512
N_MOD = 6
EPS = 1e-6
IN_COLS = 3 * D_CONV + 2 * D_LRU + N_BRANCH * D_MODEL
SPLIT_AT = (D_CONV, 2 * D_CONV, 3 * D_CONV, 3 * D_CONV + D_LRU, 3 * D_CONV + 2 * D_LRU,
            3 * D_CONV + 2 * D_LRU + D_MODEL)

kernel_name = "hybrid_conv_rglru_moe_diffusion_block"


def rmsnorm(x, g):
    xf = x.astype(jnp.float32)
    y = xf * lax.rsqrt(jnp.mean(xf * xf, axis=-1, keepdims=True) + EPS)
    return (y * g.astype(jnp.float32)).astype(x.dtype)


def modulate(h, shift, scale):
    return h * (1 + scale) + shift


def adaln_params(silu_cond, w, b):
    m = silu_cond @ w + b
    m = m.reshape(m.shape[0], 1, N_MOD, D_MODEL)
    return tuple(m[:, :, i] for i in range(N_MOD))


def seq_conv3(u, w):
    L = u.shape[1]
    up = jnp.pad(u, ((0, 0), (1, 1), (0, 0)))
    return sum(w[k] * up[:, k:k + L] for k in range(CONV_W))


def grid_conv3(u, w, rows):
    B_, L, C = u.shape
    half = C // 2
    g = u.reshape(B_, rows, GRID_W, C)
    gh = jnp.pad(g[..., :half], ((0, 0), (0, 0), (1, 1), (0, 0)))
    gv = jnp.pad(g[..., half:], ((0, 0), (1, 1), (0, 0), (0, 0)))
    yh = sum(w[k, :half] * gh[:, :, k:k + GRID_W] for k in range(CONV_W))
    yv = sum(w[k, half:] * gv[:, k:k + rows] for k in range(CONV_W))
    return jnp.concatenate([yh, yv], axis=-1).reshape(B_, L, C)


def directional_conv4(u, w, b, reverse):
    L = u.shape[1]
    k = LRU_CONV_W
    if reverse:
        up = jnp.pad(u, ((0, 0), (0, k - 1), (0, 0)))
        return b + sum(w[j] * up[:, k - 1 - j:k - 1 - j + L] for j in range(k))
    up = jnp.pad(u, ((0, 0), (k - 1, 0), (0, 0)))
    return b + sum(w[j] * up[:, j:j + L] for j in range(k))


def rglru_coeffs(v, wa, ba, wx, bx, lam):
    B_, L, C = v.shape
    vb = v.reshape(B_, L, LRU_BLOCKS, LRU_BLOCK_W)
    r = jax.nn.sigmoid(jnp.einsum('blnk,nkj->blnj', vb, wa).reshape(B_, L, C) + ba)
    i = jax.nn.sigmoid(jnp.einsum('blnk,nkj->blnj', vb, wx).reshape(B_, L, C) + bx)
    log_a = (-LRU_C * jax.nn.softplus(-lam.astype(jnp.float32))) * r.astype(jnp.float32)
    a = jnp.exp(log_a)
    b = jnp.sqrt(-jnp.expm1(2.0 * log_a)) * (i * v).astype(jnp.float32)
    return a, b


def linear_scan(a, b, h0, reverse):
    edge = -1 if reverse else 0
    b = b.at[:, edge].add(a[:, edge] * h0)

    def comb(lhs, rhs):
        return (lhs[0] * rhs[0], rhs[0] * lhs[1] + rhs[1])

    _, h = lax.associative_scan(comb, (a, b), reverse=reverse, axis=1)
    return h


def lru_direction(u, conv_w, conv_b, wa, ba, wx, bx, lam, h0, reverse):
    v = directional_conv4(u, conv_w, conv_b, reverse)
    a, b = rglru_coeffs(v, wa, ba, wx, bx, lam)
    return linear_scan(a, b, h0, reverse)


def merge_branches(cols, conv_fn, h_lru, conv_w, w_out_a, w_out_b, b_merge, w_o):
    gate_b, gate_c, xc, _, lru_gate, m_a, m_b = cols
    y_a = (gate_b * conv_fn(gate_c * xc, conv_w)) @ w_out_a
    y_b = (jax.nn.gelu(lru_gate) * h_lru.astype(lru_gate.dtype)) @ w_out_b
    merged = jax.nn.sigmoid(m_a + b_merge[0]) * y_a + jax.nn.sigmoid(m_b + b_merge[1]) * y_b
    return merged @ w_o


def moe(xt, router_w, router_b, w1, b1, w2, b2):
    T, D = xt.shape
    logits = xt.astype(jnp.float32) @ router_w.astype(jnp.float32) + router_b.astype(jnp.float32)
    top_val, top_idx = lax.top_k(logits, TOP_K)
    gates = jax.nn.softmax(top_val, axis=-1)
    flat_e = top_idx.reshape(-1)
    flat_tok = jnp.arange(T * TOP_K, dtype=jnp.int32) // TOP_K
    flat_g = gates.reshape(-1)
    order = jnp.argsort(flat_e)
    se, stok, sg = flat_e[order], flat_tok[order], flat_g[order]
    counts = jnp.bincount(flat_e, length=N_EXPERTS)
    starts = jnp.cumsum(counts) - counts
    pcounts = ((counts + MOE_BLOCK - 1) // MOE_BLOCK) * MOE_BLOCK
    pends = jnp.cumsum(pcounts)
    pstarts = pends - pcounts
    dest = pstarts[se] + (jnp.arange(T * TOP_K, dtype=jnp.int32) - starts[se])
    n_blocks = (T * TOP_K + MOE_BLOCK - 1) // MOE_BLOCK + N_EXPERTS
    n_rows = n_blocks * MOE_BLOCK
    row_tok = jnp.zeros((n_rows,), jnp.int32).at[dest].set(stok)
    row_g = jnp.zeros((n_rows,), xt.dtype).at[dest].set(sg.astype(xt.dtype))
    block_e = jnp.clip(jnp.searchsorted(pends, jnp.arange(n_blocks) * MOE_BLOCK, side='right'),
                       0, N_EXPERTS - 1)

    def run_block(args):
        tok, e = args
        h = xt[tok] @ w1[e] + b1[e]
        glu = jnp.minimum(h[:, ::2], SWIGLU_LIMIT)
        lin = jnp.clip(h[:, 1::2], -SWIGLU_LIMIT, SWIGLU_LIMIT)
        act = glu * jax.nn.sigmoid(SWIGLU_ALPHA * glu) * (lin + 1)
        return act @ w2[e] + b2[e]

    y = lax.map(run_block, (row_tok.reshape(n_blocks, MOE_BLOCK), block_e))
    y = y.reshape(n_rows, D) * row_g[:, None]
    return jax.ops.segment_sum(y, row_tok, num_segments=T)


def setup_inputs(seed: int = 0) -> dict:
    key = jax.random.key(seed)
    ks = jax.random.split(key, 32)
    f32 = jnp.float32

    def nrm(k, shape, scale):
        return jax.random.normal(k, shape, f32) * scale

    u = jax.random.uniform(ks[16], (DEPTH, 2, D_LRU), f32, 0.9, 0.999)
    a0 = u ** (1.0 / LRU_C)
    lam = jnp.log(a0) - jnp.log1p(-a0)
    return {
        "x": nrm(ks[0], (BATCH, SEQ, D_MODEL), 1.0),
        "c": nrm(ks[1], (BATCH, D_MODEL), 1.0),
        "ctx": nrm(ks[2], (BATCH, CTX_LEN, D_MODEL), 1.0),
        "c_ctx": nrm(ks[3], (D_MODEL,), 1.0),
        "w_ada": nrm(ks[4], (DEPTH, D_MODEL, N_MOD * D_MODEL), D_MODEL ** -0.5),
        "b_ada": nrm(ks[5], (DEPTH, N_MOD * D_MODEL), 0.02),
        "norm_mix": 1.0 + nrm(ks[6], (DEPTH, D_MODEL), 0.02),
        "w_in": nrm(ks[7], (DEPTH, D_MODEL, IN_COLS), D_MODEL ** -0.5),
        "conv_a_w": nrm(ks[8], (DEPTH, CONV_W, D_CONV), CONV_W ** -0.5),
        "w_out_a": nrm(ks[9], (DEPTH, D_CONV, D_MODEL), D_CONV ** -0.5),
        "lru_conv_w": nrm(ks[10], (DEPTH, 2, LRU_CONV_W, D_LRU), LRU_CONV_W ** -0.5),
        "lru_conv_b": nrm(ks[11], (DEPTH, 2, D_LRU), 0.02),
        "lru_wa": nrm(ks[12], (DEPTH, 2, LRU_BLOCKS, LRU_BLOCK_W, LRU_BLOCK_W), LRU_BLOCK_W ** -0.5),
        "lru_ba": nrm(ks[13], (DEPTH, 2, D_LRU), 0.02),
        "lru_wx": nrm(ks[14], (DEPTH, 2, LRU_BLOCKS, LRU_BLOCK_W, LRU_BLOCK_W), LRU_BLOCK_W ** -0.5),
        "lru_bx": nrm(ks[15], (DEPTH, 2, D_LRU), 0.02),
        "lru_lambda": lam,
        "w_out_b": nrm(ks[17], (DEPTH, D_LRU, D_MODEL), D_LRU ** -0.5),
        "b_merge": nrm(ks[18], (DEPTH, N_BRANCH, D_MODEL), 0.02),
        "w_o": nrm(ks[19], (DEPTH, D_MODEL, D_MODEL), D_MODEL ** -0.5),
        "norm_ffn": 1.0 + nrm(ks[20], (DEPTH, D_MODEL), 0.02),
        "router_w": nrm(ks[21], (DEPTH, D_MODEL, N_EXPERTS), D_MODEL ** -0.5),
        "router_b": nrm(ks[22], (DEPTH, N_EXPERTS), 0.01),
        "w1": nrm(ks[23], (DEPTH, N_EXPERTS, D_MODEL, 2 * D_EXPERT), D_MODEL ** -0.5),
        "b1": nrm(ks[24], (DEPTH, N_EXPERTS, 2 * D_EXPERT), 0.02),
        "w2": nrm(ks[25], (DEPTH, N_EXPERTS, D_EXPERT, D_MODEL), D_EXPERT ** -0.5),
        "b2": nrm(ks[26], (DEPTH, N_EXPERTS, D_MODEL), 0.02),
        "norm_final": 1.0 + nrm(ks[27], (D_MODEL,), 0.02),
    }


def reference(x, c, ctx, c_ctx, w_ada, b_ada, norm_mix, w_in, conv_a_w, w_out_a, lru_conv_w,
              lru_conv_b, lru_wa, lru_ba, lru_wx, lru_bx, lru_lambda, w_out_b, b_merge, w_o,
              norm_ffn, router_w, router_b, w1, b1, w2, b2, norm_final):
    B_, S, D = x.shape
    rows = S // GRID_W
    latent_conv = functools.partial(grid_conv3, rows=rows)
    silu_c = jax.nn.silu(c)
    silu_cc = jax.nn.silu(c_ctx)[None]
    split_cols = lambda p: jnp.split(p, SPLIT_AT, axis=-1)
    for l in range(DEPTH):
        last = l == DEPTH - 1
        mx = adaln_params(silu_c, w_ada[l], b_ada[l])
        mc = adaln_params(silu_cc, w_ada[l], b_ada[l])

        hx = modulate(rmsnorm(x, norm_mix[l]), mx[0], mx[1])
        hc = modulate(rmsnorm(ctx, norm_mix[l]), mc[0], mc[1])
        xs = split_cols(jnp.einsum('bld,de->ble', hx, w_in[l]))
        cs = split_cols(jnp.einsum('bld,de->ble', hc, w_in[l]))
        fwd = (lru_conv_w[l, 0], lru_conv_b[l, 0], lru_wa[l, 0], lru_ba[l, 0],
               lru_wx[l, 0], lru_bx[l, 0], lru_lambda[l, 0])
        bwd = (lru_conv_w[l, 1], lru_conv_b[l, 1], lru_wa[l, 1], lru_ba[l, 1],
               lru_wx[l, 1], lru_bx[l, 1], lru_lambda[l, 1])
        h_zero = jnp.zeros((B_, D_LRU), jnp.float32)
        hcf = lru_direction(cs[3], *fwd, h0=h_zero, reverse=False)
        hcb = lru_direction(cs[3], *bwd, h0=h_zero, reverse=True)
        hxf = lru_direction(xs[3], *fwd, h0=hcf[:, -1], reverse=False)
        hxb = lru_direction(xs[3], *bwd, h0=hcb[:, 0], reverse=True)
        mix_x = merge_branches(xs, latent_conv, hxf + hxb, conv_a_w[l], w_out_a[l], w_out_b[l],
                               b_merge[l], w_o[l])
        x = x + mx[2] * mix_x
        if not last:
            mix_c = merge_branches(cs, seq_conv3, hcf + hcb, conv_a_w[l], w_out_a[l], w_out_b[l],
                                   b_merge[l], w_o[l])
            ctx = ctx + mc[2] * mix_c

        fx = modulate(rmsnorm(x, norm_ffn[l]), mx[3], mx[4]).reshape(-1, D)
        if last:
            y = moe(fx, router_w[l], router_b[l], w1[l], b1[l], w2[l], b2[l])
            x = x + mx[5] * y.reshape(x.shape)
        else:
            fc = modulate(rmsnorm(ctx, norm_ffn[l]), mc[3], mc[4]).reshape(-1, D)
            y = moe(jnp.concatenate([fx, fc], axis=0), router_w[l], router_b[l], w1[l], b1[l],
                    w2[l], b2[l])
            n_lat = fx.shape[0]
            x = x + mx[5] * y[:n_lat].reshape(x.shape)
            ctx = ctx + mc[5] * y[n_lat:].reshape(ctx.shape)
    return rmsnorm(x, norm_final)
```

```python
import functools

import jax
import jax.numpy as jnp
from jax import lax
from jax.experimental import pallas as pl
from jax.experimental.pallas import tpu as pltpu

F32 = jnp.float32
BF16 = jnp.bfloat16

EPS = 1e-6
GRID_W = 64
CONV_W = 3
LRU_CONV_W = 4
LRU_BLOCK_W = 64
LRU_C = 8.0
N_MOD = 6
TOP_K = 4
MOE_BLOCK = 512
SWIGLU_LIMIT = 7.0
SWIGLU_ALPHA = 1.702

MXU_W = 256
SUBLANES = 8
VMEM_LIMIT = 56 * 1024 * 1024
TOKEN_TILE = 512
LRU_CHUNK = 512


def _cparams(sem):
    return pltpu.CompilerParams(dimension_semantics=sem, vmem_limit_bytes=VMEM_LIMIT)


def _full(shape):
    nd = len(shape)
    return pl.BlockSpec(shape, lambda *_: (0,) * nd)


def _adaln_kernel(cond_ref, w_ref, b_ref, o_ref):
    c = cond_ref[...]
    s = c * jax.nn.sigmoid(c)
    o_ref[...] = jnp.dot(s, w_ref[...], preferred_element_type=F32,
                         precision=lax.Precision.HIGHEST) + b_ref[...]


def adaln(cond, w, b):
    R, D = cond.shape
    N = w.shape[1]
    tn = D
    return pl.pallas_call(
        _adaln_kernel,
        grid=(N // tn,),
        in_specs=[pl.BlockSpec((R, D), lambda j: (0, 0)),
                  pl.BlockSpec((D, tn), lambda j: (0, j)),
                  pl.BlockSpec((1, tn), lambda j: (0, j))],
        out_specs=pl.BlockSpec((R, tn), lambda j: (0, j)),
        out_shape=jax.ShapeDtypeStruct((R, N), F32),
        compiler_params=_cparams(("arbitrary",)),
        name="adaln",
    )(cond, w, b.reshape(1, N))


def _inproj_kernel(x_ref, g_ref, shift_ref, scale_ref, w_ref, *out_refs):
    xf = x_ref[...]
    ms = jnp.mean(xf * xf, axis=-1, keepdims=True)
    y = xf * lax.rsqrt(ms + EPS) * g_ref[...]
    h = (y * (1.0 + scale_ref[...]) + shift_ref[...]).astype(BF16)
    for j, o in enumerate(out_refs):
        n = o.shape[-1]
        o[...] = jnp.dot(h, w_ref[:, j * n:(j + 1) * n],
                         preferred_element_type=F32).astype(o.dtype)


def inproj(x, g, shift, scale, w, n_groups):
    B, L, D = x.shape
    tm = min(TOKEN_TILE, L)
    n = w.shape[1] // n_groups
    tile = pl.BlockSpec((None, tm, D), lambda b, i: (b, i, 0))
    vec = pl.BlockSpec((None, 1, D), lambda b, i: (b, 0, 0))
    return pl.pallas_call(
        _inproj_kernel,
        grid=(B, L // tm),
        in_specs=[tile, _full((1, D)), vec, vec,
                  pl.BlockSpec(w.shape, lambda b, i: (0, 0), pipeline_mode=pl.Buffered(1))],
        out_specs=[pl.BlockSpec((None, tm, n), lambda b, i: (b, i, 0))] * n_groups,
        out_shape=[jax.ShapeDtypeStruct((B, L, n), BF16)] * n_groups,
        compiler_params=_cparams(("arbitrary", "arbitrary")),
        name="inproj",
    )(x, g.reshape(1, D), shift, scale, w)


def _softplus(z):
    return jnp.maximum(z, 0.0) + jnp.log1p(jnp.exp(-jnp.abs(z)))


def _lru_kernel(uf_ref, ub_ref, cw_ref, cb_ref, wg_ref, ba_ref, bx_ref, lam_ref, h0_ref,
                hf_ref, hb_ref, hlast_ref,
                uext_f, uext_b, a_s, b_s, h_s, carry_f, carry_b):
    j = pl.program_id(1)
    tc, C = uf_ref.shape
    halo = SUBLANES
    n_groups = C // MXU_W
    nblk = tc // SUBLANES

    @pl.when(j == 0)
    def _():
        uext_f[0:halo, :] = jnp.zeros((halo, C), F32)
        uext_b[tc:tc + halo, :] = jnp.zeros((halo, C), F32)
        carry_f[...] = h0_ref[0]
        carry_b[...] = h0_ref[1]

    row = lax.broadcasted_iota(jnp.int32, (SUBLANES, C), 0)

    def one_dir(d, u_ref, uext, o_ref, carry, reverse):
        if reverse:
            uext[0:tc, :] = u_ref[...].astype(F32)
            offs = [LRU_CONV_W - 1 - q for q in range(LRU_CONV_W)]
        else:
            uext[halo:halo + tc, :] = u_ref[...].astype(F32)
            offs = [halo - (LRU_CONV_W - 1) + q for q in range(LRU_CONV_W)]
        v = cb_ref[pl.ds(d, 1), :]
        for q in range(LRU_CONV_W):
            v = v + cw_ref[d, pl.ds(q, 1), :] * uext[pl.ds(offs[q], tc), :]
        if reverse:
            uext[tc:tc + halo, :] = uext[0:halo, :]
        else:
            uext[0:halo, :] = uext[tc:tc + halo, :]

        c8 = -LRU_C * _softplus(-lam_ref[pl.ds(d, 1), :])
        vb = v.astype(BF16)
        for g in range(n_groups):
            cols = slice(g * MXU_W, (g + 1) * MXU_W)
            rg = jnp.dot(vb[:, cols], wg_ref[d, g], preferred_element_type=F32)
            r = jax.nn.sigmoid(rg[:, :MXU_W] + ba_ref[pl.ds(d, 1), cols])
            i = jax.nn.sigmoid(rg[:, MXU_W:] + bx_ref[pl.ds(d, 1), cols])
            log_a = c8[:, cols] * r
            a = jnp.exp(log_a)
            a_s[:, cols] = a
            b_s[:, cols] = jnp.sqrt(-jnp.tanh(log_a) * (a * a + 1.0)) * (i * v[:, cols])

        def body(k, h):
            kk = (nblk - 1 - k) if reverse else k
            sl = pl.ds(pl.multiple_of(kk * SUBLANES, SUBLANES), SUBLANES)
            a = a_s[sl, :]
            b = b_s[sl, :]
            for s in (1, 2, 4):
                if reverse:
                    m = row < SUBLANES - s
                    sh = SUBLANES - s
                else:
                    m = row >= s
                    sh = s
                a_sh = jnp.where(m, pltpu.roll(a, sh, 0), 1.0)
                b_sh = jnp.where(m, pltpu.roll(b, sh, 0), 0.0)
                b = a * b_sh + b
                a = a * a_sh
            hblk = a * h + b
            h_s[sl, :] = hblk
            return hblk[0:1, :] if reverse else hblk[SUBLANES - 1:SUBLANES, :]

        h = lax.fori_loop(0, nblk, body, carry[...])
        carry[...] = h
        o_ref[...] = h_s[...].astype(o_ref.dtype)

    one_dir(0, uf_ref, uext_f, hf_ref, carry_f, False)
    one_dir(1, ub_ref, uext_b, hb_ref, carry_b, True)
    hlast_ref[0] = carry_f[...]
    hlast_ref[1] = carry_b[...]


def lru(u, conv_w, conv_b, wg, ba, bx, lam, h0, out_dtype):
    B, L, C = u.shape
    tc = min(LRU_CHUNK, L)
    n = L // tc
    blk_f = pl.BlockSpec((None, tc, C), lambda b, j: (b, j, 0))
    blk_b = pl.BlockSpec((None, tc, C), lambda b, j: (b, n - 1 - j, 0))
    state = pl.BlockSpec((2, None, 1, C), lambda b, j: (0, b, 0, 0))
    return pl.pallas_call(
        _lru_kernel,
        grid=(B, n),
        in_specs=[blk_f, blk_b, _full(conv_w.shape), _full(conv_b.shape), _full(wg.shape),
                  _full(ba.shape), _full(bx.shape), _full(lam.shape), state],
        out_specs=[blk_f, blk_b, state],
        out_shape=[jax.ShapeDtypeStruct((B, L, C), out_dtype),
                   jax.ShapeDtypeStruct((B, L, C), out_dtype),
                   jax.ShapeDtypeStruct((2, B, 1, C), F32)],
        scratch_shapes=[pltpu.VMEM((tc + SUBLANES, C), F32), pltpu.VMEM((tc + SUBLANES, C), F32),
                        pltpu.VMEM((tc, C), F32), pltpu.VMEM((tc, C), F32), pltpu.VMEM((tc, C), F32),
                        pltpu.VMEM((1, C), F32), pltpu.VMEM((1, C), F32)],
        compiler_params=_cparams(("arbitrary", "arbitrary")),
        name="lru",
    )(u, u, conv_w, conv_b, wg, ba, bx, lam, h0)


def _gate_weights(wa, wx):
    per = MXU_W // LRU_BLOCK_W
    nd, nb, k, _ = wa.shape
    eye = jnp.eye(per, dtype=wa.dtype)

    def bd(w):
        w = w.reshape(nd, nb // per, per, k, k)
        return jnp.einsum('dgpkj,pq->dgpkqj', w, eye).reshape(nd, nb // per, MXU_W, MXU_W)

    return jnp.concatenate([bd(wa), bd(wx)], axis=-1).astype(BF16)


def _merge_kernel(gb_ref, gc_ref, xc_ref, gcp_ref, xcp_ref, gcn_ref, xcn_ref,
                  lg_ref, hf_ref, hb_ref, ma_ref, mb_ref, x_ref,
                  cw_ref, woa_ref, wob_ref, wo_ref, bm_ref, gate_ref, nf_ref, sh_ref, sc_ref,
                  rwt_ref, rb_ref,
                  x1_ref, fx_ref, idx_ref, gts_ref):
    i = pl.program_id(1)
    n = pl.num_programs(1)
    tm, C = gc_ref.shape
    half = C // 2
    E = rwt_ref.shape[0]

    u = gc_ref[...].astype(F32) * xc_ref[...].astype(F32)
    uh = u[:, :half]
    col = lax.broadcasted_iota(jnp.int32, (tm, half), 0) % GRID_W
    left = jnp.where(col >= 1, pltpu.roll(uh, 1, 0), 0.0)
    right = jnp.where(col <= GRID_W - 2, pltpu.roll(uh, tm - 1, 0), 0.0)
    yh = (cw_ref[0:1, :half] * left + cw_ref[1:2, :half] * uh + cw_ref[2:3, :half] * right)
    uv = u[:, half:]
    prev = gcp_ref[...].astype(F32) * xcp_ref[...].astype(F32)
    nxt = gcn_ref[...].astype(F32) * xcn_ref[...].astype(F32)
    prev = jnp.where(i > 0, prev, 0.0)
    nxt = jnp.where(i < n - 1, nxt, 0.0)
    up = jnp.concatenate([prev, uv[:tm - GRID_W]], axis=0)
    down = jnp.concatenate([uv[GRID_W:], nxt], axis=0)
    yv = (cw_ref[0:1, half:] * up + cw_ref[1:2, half:] * uv + cw_ref[2:3, half:] * down)

    gb = gb_ref[...].astype(F32)
    za_h = (gb[:, :half] * yh).astype(BF16)
    za_v = (gb[:, half:] * yv).astype(BF16)
    y_a = (jnp.dot(za_h, woa_ref[:half, :], preferred_element_type=F32)
           + jnp.dot(za_v, woa_ref[half:, :], preferred_element_type=F32))

    h_lru = hf_ref[...].astype(F32) + hb_ref[...].astype(F32)
    zb = (jax.nn.gelu(lg_ref[...].astype(F32), approximate=True) * h_lru).astype(BF16)
    y_b = jnp.dot(zb, wob_ref[...], preferred_element_type=F32)

    merged = (jax.nn.sigmoid(ma_ref[...].astype(F32) + bm_ref[0:1, :]) * y_a
              + jax.nn.sigmoid(mb_ref[...].astype(F32) + bm_ref[1:2, :]) * y_b)
    mix = jnp.dot(merged.astype(BF16), wo_ref[...], preferred_element_type=F32)
    x1 = x_ref[...] + gate_ref[...] * mix
    x1_ref[...] = x1

    ms = jnp.mean(x1 * x1, axis=-1, keepdims=True)
    fx = (x1 * lax.rsqrt(ms + EPS) * nf_ref[...]) * (1.0 + sc_ref[...]) + sh_ref[...]
    fx_ref[...] = fx

    logits = lax.dot_general(rwt_ref[...], fx, (((1,), (1,)), ((), ())),
                             preferred_element_type=F32,
                             precision=lax.Precision.HIGHEST) + rb_ref[...]
    erow = lax.broadcasted_iota(jnp.int32, (E, tm), 0).astype(F32)
    vals = []
    for k in range(TOP_K):
        m = jnp.max(logits, axis=0, keepdims=True)
        am = jnp.min(jnp.where(logits == m, erow, float(E)), axis=0, keepdims=True)
        vals.append(m)
        idx_ref[k:k + 1, :] = am.astype(jnp.int32)
        logits = jnp.where(erow == am, -jnp.inf, logits)
    ex = [jnp.exp(v - vals[0]) for v in vals]
    den = ex[0] + ex[1] + ex[2] + ex[3]
    for k in range(TOP_K):
        gts_ref[k:k + 1, :] = ex[k] / den


def merge(cols, hf, hb, x, conv_w, w_out_a, w_out_b, w_o, b_merge, gate, norm_ffn, shift, scale,
          router_wt, router_b):
    gb, gc, xc, _, lg, ma, mb = cols
    B, S, D = x.shape
    C = gc.shape[-1]
    half = C // 2
    E = router_wt.shape[0]
    tm = min(TOKEN_TILE, S)
    n = S // tm
    r = tm // GRID_W
    n_rows = S // GRID_W
    tile = lambda w: pl.BlockSpec((None, tm, w), lambda b, i: (b, i, 0))
    prev = pl.BlockSpec((None, GRID_W, half), lambda b, i: (b, jnp.maximum(i * r - 1, 0), 1))
    nxt = pl.BlockSpec((None, GRID_W, half),
                       lambda b, i: (b, jnp.minimum((i + 1) * r, n_rows - 1), 1))
    vec = pl.BlockSpec((None, 1, D), lambda b, i: (b, 0, 0))
    lane_out = pl.BlockSpec((None, TOP_K, tm), lambda b, i: (b, 0, i))
    return pl.pallas_call(
        _merge_kernel,
        grid=(B, n),
        in_specs=[tile(C), tile(C), tile(C), prev, prev, nxt, nxt,
                  tile(C), tile(C), tile(C), tile(D), tile(D), tile(D),
                  _full(conv_w.shape), _full(w_out_a.shape), _full(w_out_b.shape),
                  _full(w_o.shape), _full(b_merge.shape), vec, _full((1, D)), vec, vec,
                  _full(router_wt.shape), _full((E, 1))],
        out_specs=[tile(D), tile(D), lane_out, lane_out],
        out_shape=[jax.ShapeDtypeStruct((B, S, D), F32), jax.ShapeDtypeStruct((B, S, D), F32),
                   jax.ShapeDtypeStruct((B, TOP_K, S), jnp.int32),
                   jax.ShapeDtypeStruct((B, TOP_K, S), F32)],
        compiler_params=_cparams(("arbitrary", "arbitrary")),
        name="merge",
    )(gb, gc, xc, gc, xc, gc, xc, lg, hf, hb, ma, mb, x,
      conv_w, w_out_a, w_out_b, w_o, b_merge, gate, norm_ffn.reshape(1, D), shift, scale,
      router_wt, router_b.reshape(E, 1))


def _selection(idx_ref, n_experts):
    tp = idx_ref.shape[1]
    idx = idx_ref[...]
    erow = lax.broadcasted_iota(jnp.int32, (n_experts, tp), 0)
    sel = [erow == idx[k:k + 1, :] for k in range(TOP_K)]
    onehot = sum(s.astype(F32) for s in sel)
    tile_cnt = jnp.sum(onehot, axis=1, keepdims=True).astype(jnp.int32)
    return sel, onehot, tile_cnt


def _count_kernel(idx_ref, cnt_ref):
    first = (pl.program_id(0) == 0) & (pl.program_id(1) == 0)
    _, _, tile_cnt = _selection(idx_ref, cnt_ref.shape[0])

    @pl.when(first)
    def _():
        cnt_ref[...] = jnp.zeros_like(cnt_ref)

    cnt_ref[...] += tile_cnt


def _plan_kernel(idx_ref, cnt_ref, dest_ref, blk_ref, base_s, tri_s):
    first = (pl.program_id(0) == 0) & (pl.program_id(1) == 0)
    tp = idx_ref.shape[1]
    E = cnt_ref.shape[0]
    n_blk = blk_ref.shape[1]
    shift = MOE_BLOCK.bit_length() - 1
    sel, onehot, tile_cnt = _selection(idx_ref, E)

    @pl.when(first)
    def _():
        cnt = cnt_ref[...]
        padded = ((cnt + (MOE_BLOCK - 1)) >> shift) << shift
        r_i = lax.broadcasted_iota(jnp.int32, (E, E), 0)
        c_i = lax.broadcasted_iota(jnp.int32, (E, E), 1)
        padded_row = jnp.sum(jnp.where(r_i == c_i, padded, 0).astype(F32), axis=0,
                             keepdims=True)
        pstart = jnp.sum(jnp.where(c_i < r_i, padded_row, 0.0), axis=1,
                         keepdims=True).astype(jnp.int32)
        base_s[...] = pstart
        bs = lax.broadcasted_iota(jnp.int32, (E, n_blk), 1) * MOE_BLOCK
        inb = (bs >= pstart) & (bs < pstart + padded)
        e_i = lax.broadcasted_iota(jnp.int32, (E, n_blk), 0)
        valid = jnp.clip(cnt - (bs - pstart), 0, MOE_BLOCK)
        blk_ref[0:1, :] = jnp.sum(jnp.where(inb, e_i, 0).astype(F32), axis=0,
                                  keepdims=True).astype(jnp.int32)
        blk_ref[1:2, :] = jnp.sum(jnp.where(inb, valid, 0).astype(F32), axis=0,
                                  keepdims=True).astype(jnp.int32)
        t_r = lax.broadcasted_iota(jnp.int32, (tp, tp), 0)
        t_c = lax.broadcasted_iota(jnp.int32, (tp, tp), 1)
        tri_s[...] = (t_r < t_c).astype(BF16)

    pref = jnp.dot(onehot.astype(BF16), tri_s[...], preferred_element_type=F32)
    tot = pref + base_s[...].astype(F32)
    for k in range(TOP_K):
        dest_ref[k:k + 1, :] = jnp.sum(jnp.where(sel[k], tot, 0.0), axis=0,
                                       keepdims=True).astype(jnp.int32)
    base_s[...] += tile_cnt


def plan(idx, n_experts, n_blk):
    B, K, S = idx.shape
    tp = min(TOKEN_TILE, S)
    lane = pl.BlockSpec((None, K, tp), lambda b, i: (b, 0, i))
    cnt = pl.pallas_call(
        _count_kernel,
        grid=(B, S // tp),
        in_specs=[lane],
        out_specs=_full((n_experts, 1)),
        out_shape=jax.ShapeDtypeStruct((n_experts, 1), jnp.int32),
        compiler_params=_cparams(("arbitrary", "arbitrary")),
        name="count",
    )(idx)
    return pl.pallas_call(
        _plan_kernel,
        grid=(B, S // tp),
        in_specs=[lane, _full((n_experts, 1))],
        out_specs=[lane, _full((2, n_blk))],
        out_shape=[jax.ShapeDtypeStruct((B, K, S), jnp.int32),
                   jax.ShapeDtypeStruct((2, n_blk), jnp.int32)],
        scratch_shapes=[pltpu.VMEM((n_experts, 1), jnp.int32), pltpu.VMEM((tp, tp), BF16)],
        compiler_params=_cparams(("arbitrary", "arbitrary")),
        name="plan",
    )(idx, cnt)


def _row_copy(src, s, dst, d, sem):
    return pltpu.make_async_copy(src.at[pl.ds(s, 1), :], dst.at[pl.ds(d, 1), :], sem)


def _dispatch_kernel(dest_ref, fx_hbm, xs_hbm, sem):
    tq = dest_ref.shape[1]
    base = (pl.program_id(0) * pl.num_programs(1) + pl.program_id(1)) * tq

    def issue(t, c):
        for k in range(TOP_K):
            _row_copy(fx_hbm, base + t, xs_hbm, dest_ref[k, t], sem).start()
        return c

    lax.fori_loop(0, tq, issue, 0)

    def drain(t, c):
        for k in range(TOP_K):
            _row_copy(fx_hbm, base + t, xs_hbm, dest_ref[k, t], sem).wait()
        return c

    lax.fori_loop(0, tq, drain, 0)


def dispatch(dest, fx, n_rows):
    B, K, S = dest.shape
    T, D = fx.shape
    tq = min(TOKEN_TILE, S)
    return pl.pallas_call(
        _dispatch_kernel,
        grid=(B, S // tq),
        in_specs=[pl.BlockSpec((None, K, tq), lambda b, i: (b, 0, i), memory_space=pltpu.SMEM),
                  pl.BlockSpec(memory_space=pl.ANY)],
        out_specs=pl.BlockSpec(memory_space=pl.ANY),
        out_shape=jax.ShapeDtypeStruct((n_rows, D), fx.dtype),
        scratch_shapes=[pltpu.SemaphoreType.DMA(())],
        compiler_params=_cparams(("arbitrary", "arbitrary")),
        name="dispatch",
    )(dest, fx)


def _expert_kernel(blk_ref, x_ref, w1g_ref, w1l_ref, b1g_ref, b1l_ref, w2_ref, b2_ref, y_ref):
    j = pl.program_id(0)
    nvalid = blk_ref[1, j]

    @pl.when(nvalid > 0)
    def _():
        rows = lax.broadcasted_iota(jnp.int32, x_ref.shape, 0)
        x = jnp.where(rows < nvalid, x_ref[...], 0.0).astype(BF16)
        hg = jnp.dot(x, w1g_ref[...], preferred_element_type=F32) + b1g_ref[...]
        hl = jnp.dot(x, w1l_ref[...], preferred_element_type=F32) + b1l_ref[...]
        glu = jnp.minimum(hg, SWIGLU_LIMIT)
        lin = jnp.clip(hl, -SWIGLU_LIMIT, SWIGLU_LIMIT)
        act = glu * jax.nn.sigmoid(SWIGLU_ALPHA * glu) * (lin + 1.0)
        y_ref[...] = jnp.dot(act.astype(BF16), w2_ref[...], preferred_element_type=F32) + b2_ref[...]

    @pl.when(nvalid == 0)
    def _():
        y_ref[...] = jnp.zeros_like(y_ref)


def experts(blk, xs, w1g, w1l, b1g, b1l, w2, b2):
    n_rows, D = xs.shape
    E, _, F = w1g.shape
    n_blk = n_rows // MOE_BLOCK
    rows = pl.BlockSpec((MOE_BLOCK, D), lambda j, blk: (j, 0))
    wsel = lambda a, b: pl.BlockSpec((None, a, b), lambda j, blk: (blk[0, j], 0, 0))
    return pl.pallas_call(
        _expert_kernel,
        grid_spec=pltpu.PrefetchScalarGridSpec(
            num_scalar_prefetch=1,
            grid=(n_blk,),
            in_specs=[rows, wsel(D, F), wsel(D, F), wsel(1, F), wsel(1, F), wsel(F, D), wsel(1, D)],
            out_specs=rows),
        out_shape=jax.ShapeDtypeStruct((n_rows, D), F32),
        compiler_params=_cparams(("arbitrary",)),
        name="experts",
    )(blk, xs, w1g, w1l, b1g, b1l, w2, b2)


def _combine_kernel(dest_ref, ys_hbm, gts_ref, x1_ref, gate_ref, nf_ref, o_ref, buf, sem):
    tq = x1_ref.shape[0]

    def issue(t, c):
        for k in range(TOP_K):
            pltpu.make_async_copy(ys_hbm.at[pl.ds(dest_ref[k, t], 1), :],
                                  buf.at[k, pl.ds(t, 1), :], sem).start()
        return c

    lax.fori_loop(0, tq, issue, 0)

    def drain(t, c):
        for k in range(TOP_K):
            pltpu.make_async_copy(ys_hbm.at[pl.ds(dest_ref[k, t], 1), :],
                                  buf.at[k, pl.ds(t, 1), :], sem).wait()
        return c

    lax.fori_loop(0, tq, drain, 0)

    g = gts_ref[...]
    gt = jnp.concatenate([g, jnp.zeros_like(g)], axis=0).T
    y = gt[:, 0:1] * buf[0]
    for k in range(1, TOP_K):
        y = y + gt[:, k:k + 1] * buf[k]
    xo = x1_ref[...] + gate_ref[...] * y
    ms = jnp.mean(xo * xo, axis=-1, keepdims=True)
    o_ref[...] = xo * lax.rsqrt(ms + EPS) * nf_ref[...]


def combine(dest, ys, gts, x1, gate, norm_final):
    B, S, D = x1.shape
    K = dest.shape[1]
    tq = min(TOKEN_TILE, S)
    lane = lambda ms: pl.BlockSpec((None, K, tq), lambda b, i: (b, 0, i), memory_space=ms)
    tile = pl.BlockSpec((None, tq, D), lambda b, i: (b, i, 0))
    return pl.pallas_call(
        _combine_kernel,
        grid=(B, S // tq),
        in_specs=[lane(pltpu.SMEM), pl.BlockSpec(memory_space=pl.ANY), lane(pltpu.VMEM), tile,
                  pl.BlockSpec((None, 1, D), lambda b, i: (b, 0, 0)), _full((1, D))],
        out_specs=tile,
        out_shape=jax.ShapeDtypeStruct((B, S, D), F32),
        scratch_shapes=[pltpu.VMEM((K, tq, D), F32), pltpu.SemaphoreType.DMA(())],
        compiler_params=_cparams(("arbitrary", "arbitrary")),
        name="combine",
    )(dest, ys, gts, x1, gate, norm_final.reshape(1, D))


def kernel(x, c, ctx, c_ctx, w_ada, b_ada, norm_mix, w_in, conv_a_w, w_out_a, lru_conv_w,
           lru_conv_b, lru_wa, lru_ba, lru_wx, lru_bx, lru_lambda, w_out_b, b_merge, w_o,
           norm_ffn, router_w, router_b, w1, b1, w2, b2, norm_final):
    B, S, D = x.shape
    depth = w_ada.shape[0]
    assert depth == 1, "single-layer block"
    l = 0
    E = router_w.shape[-1]
    n_groups = w_in.shape[-1] // D
    lru_col = 3

    pad = -(B + 1) % SUBLANES
    cond = jnp.concatenate([c, c_ctx[None], jnp.zeros((pad, D), F32)], axis=0)
    mod = adaln(cond, w_ada[l], b_ada[l])
    mx = [mod[:B, i * D:(i + 1) * D].reshape(B, 1, D) for i in range(N_MOD)]
    mc = [jnp.broadcast_to(mod[B:B + 1, i * D:(i + 1) * D].reshape(1, 1, D), (B, 1, D))
          for i in range(2)]

    w_in_b = w_in[l].astype(BF16)
    wg = _gate_weights(lru_wa[l], lru_wx[l])
    lru_args = (lru_conv_w[l], lru_conv_b[l], wg, lru_ba[l], lru_bx[l], lru_lambda[l])

    (cs_lru,) = inproj(ctx, norm_mix[l], mc[0], mc[1], w_in_b[:, lru_col * D:(lru_col + 1) * D], 1)
    h_zero = jnp.zeros((2, B, 1, D), F32)
    _, _, h_ctx = lru(cs_lru, *lru_args, h_zero, BF16)

    cols = inproj(x, norm_mix[l], mx[0], mx[1], w_in_b, n_groups)
    hf, hb, _ = lru(cols[lru_col], *lru_args, h_ctx, BF16)

    x1, fx, idx, gts = merge(cols, hf, hb, x, conv_a_w[l], w_out_a[l].astype(BF16),
                             w_out_b[l].astype(BF16), w_o[l].astype(BF16), b_merge[l], mx[2],
                             norm_ffn[l], mx[3], mx[4], router_w[l].T, router_b[l])

    T = B * S
    n_blk = (T * TOP_K + MOE_BLOCK - 1) // MOE_BLOCK + E
    dest, blk = plan(idx, E, n_blk)
    xs = dispatch(dest, fx.reshape(T, D), n_blk * MOE_BLOCK)
    w1l_ = w1[l]
    ys = experts(blk, xs, w1l_[:, :, 0::2].astype(BF16), w1l_[:, :, 1::2].astype(BF16),
                 b1[l][:, None, 0::2], b1[l][:, None, 1::2], w2[l].astype(BF16), b2[l][:, None, :])
    return combine(dest, ys, gts, x1, mx[5], norm_final)
```

```python
import functools

import jax
import jax.numpy as jnp
from jax import lax
from jax.experimental import pallas as pl
from jax.experimental.pallas import tpu as pltpu

F32 = jnp.float32
BF16 = jnp.bfloat16

EPS = 1e-6
GRID_W = 64
CONV_W = 3
LRU_CONV_W = 4
LRU_BLOCK_W = 64
LRU_C = 8.0
N_MOD = 6
TOP_K = 4
MOE_BLOCK = 512
SWIGLU_LIMIT = 7.0
SWIGLU_ALPHA = 1.702

MXU_W = 256
SUBLANES = 8
VMEM_LIMIT = 56 * 1024 * 1024
TOKEN_TILE = 512
LRU_CHUNK = 512


def _cparams(sem):
    return pltpu.CompilerParams(dimension_semantics=sem, vmem_limit_bytes=VMEM_LIMIT)


def _full(shape):
    nd = len(shape)
    return pl.BlockSpec(shape, lambda *_: (0,) * nd)


def _adaln_kernel(cond_ref, w_ref, b_ref, o_ref):
    c = cond_ref[...]
    s = c * jax.nn.sigmoid(c)
    o_ref[...] = jnp.dot(s, w_ref[...], preferred_element_type=F32,
                         precision=lax.Precision.HIGHEST) + b_ref[...]


def adaln(cond, w, b):
    R, D = cond.shape
    N = w.shape[1]
    tn = D
    return pl.pallas_call(
        _adaln_kernel,
        grid=(N // tn,),
        in_specs=[pl.BlockSpec((R, D), lambda j: (0, 0)),
                  pl.BlockSpec((D, tn), lambda j: (0, j)),
                  pl.BlockSpec((1, tn), lambda j: (0, j))],
        out_specs=pl.BlockSpec((R, tn), lambda j: (0, j)),
        out_shape=jax.ShapeDtypeStruct((R, N), F32),
        compiler_params=_cparams(("arbitrary",)),
        name="adaln",
    )(cond, w, b.reshape(1, N))


def _inproj_kernel(x_ref, g_ref, shift_ref, scale_ref, w_ref, *out_refs):
    xf = x_ref[...]
    ms = jnp.mean(xf * xf, axis=-1, keepdims=True)
    y = xf * lax.rsqrt(ms + EPS) * g_ref[...]
    h = (y * (1.0 + scale_ref[...]) + shift_ref[...]).astype(BF16)
    for j, o in enumerate(out_refs):
        n = o.shape[-1]
        o[...] = jnp.dot(h, w_ref[:, j * n:(j + 1) * n],
                         preferred_element_type=F32).astype(o.dtype)


def inproj(x, g, shift, scale, w, n_groups):
    B, L, D = x.shape
    tm = min(TOKEN_TILE, L)
    n = w.shape[1] // n_groups
    tile = pl.BlockSpec((None, tm, D), lambda b, i: (b, i, 0))
    vec = pl.BlockSpec((None, 1, D), lambda b, i: (b, 0, 0))
    return pl.pallas_call(
        _inproj_kernel,
        grid=(B, L // tm),
        in_specs=[tile, _full((1, D)), vec, vec,
                  pl.BlockSpec(w.shape, lambda b, i: (0, 0), pipeline_mode=pl.Buffered(1))],
        out_specs=[pl.BlockSpec((None, tm, n), lambda b, i: (b, i, 0))] * n_groups,
        out_shape=[jax.ShapeDtypeStruct((B, L, n), BF16)] * n_groups,
        compiler_params=_cparams(("arbitrary", "arbitrary")),
        name="inproj",
    )(x, g.reshape(1, D), shift, scale, w)


def _softplus(z):
    return jnp.maximum(z, 0.0) + jnp.log1p(jnp.exp(-jnp.abs(z)))


def _lru_kernel(uf_ref, ub_ref, cw_ref, cb_ref, wg_ref, ba_ref, bx_ref, lam_ref, h0_ref,
                hf_ref, hb_ref, hlast_ref,
                uext_f, uext_b, a_s, b_s, h_s, carry_f, carry_b):
    j = pl.program_id(1)
    tc, C = uf_ref.shape
    halo = SUBLANES
    n_groups = C // MXU_W
    nblk = tc // SUBLANES

    @pl.when(j == 0)
    def _():
        uext_f[0:halo, :] = jnp.zeros((halo, C), F32)
        uext_b[tc:tc + halo, :] = jnp.zeros((halo, C), F32)
        carry_f[...] = h0_ref[0]
        carry_b[...] = h0_ref[1]

    row = lax.broadcasted_iota(jnp.int32, (SUBLANES, C), 0)

    def one_dir(d, u_ref, uext, o_ref, carry, reverse):
        if reverse:
            uext[0:tc, :] = u_ref[...].astype(F32)
            offs = [LRU_CONV_W - 1 - q for q in range(LRU_CONV_W)]
        else:
            uext[halo:halo + tc, :] = u_ref[...].astype(F32)
            offs = [halo - (LRU_CONV_W - 1) + q for q in range(LRU_CONV_W)]
        v = cb_ref[pl.ds(d, 1), :]
        for q in range(LRU_CONV_W):
            v = v + cw_ref[d, pl.ds(q, 1), :] * uext[pl.ds(offs[q], tc), :]
        if reverse:
            uext[tc:tc + halo, :] = uext[0:halo, :]
        else:
            uext[0:halo, :] = uext[tc:tc + halo, :]

        c8 = -LRU_C * _softplus(-lam_ref[pl.ds(d, 1), :])
        vb = v.astype(BF16)
        for g in range(n_groups):
            cols = slice(g * MXU_W, (g + 1) * MXU_W)
            rg = jnp.dot(vb[:, cols], wg_ref[d, g], preferred_element_type=F32)
            r = jax.nn.sigmoid(rg[:, :MXU_W] + ba_ref[pl.ds(d, 1), cols])
            i = jax.nn.sigmoid(rg[:, MXU_W:] + bx_ref[pl.ds(d, 1), cols])
            log_a = c8[:, cols] * r
            a = jnp.exp(log_a)
            a_s[:, cols] = a
            b_s[:, cols] = jnp.sqrt(-jnp.tanh(log_a) * (a * a + 1.0)) * (i * v[:, cols])

        def body(k, h):
            kk = (nblk - 1 - k) if reverse else k
            sl = pl.ds(pl.multiple_of(kk * SUBLANES, SUBLANES), SUBLANES)
            a = a_s[sl, :]
            b = b_s[sl, :]
            for s in (1, 2, 4):
                if reverse:
                    m = row < SUBLANES - s
                    sh = SUBLANES - s
                else:
                    m = row >= s
                    sh = s
                a_sh = jnp.where(m, pltpu.roll(a, sh, 0), 1.0)
                b_sh = jnp.where(m, pltpu.roll(b, sh, 0), 0.0)
                b = a * b_sh + b
                a = a * a_sh
            hblk = a * h + b
            h_s[sl, :] = hblk
            return hblk[0:1, :] if reverse else hblk[SUBLANES - 1:SUBLANES, :]

        h = lax.fori_loop(0, nblk, body, carry[...])
        carry[...] = h
        o_ref[...] = h_s[...].astype(o_ref.dtype)

    one_dir(0, uf_ref, uext_f, hf_ref, carry_f, False)
    one_dir(1, ub_ref, uext_b, hb_ref, carry_b, True)
    hlast_ref[0] = carry_f[...]
    hlast_ref[1] = carry_b[...]


def lru(u, conv_w, conv_b, wg, ba, bx, lam, h0, out_dtype):
    B, L, C = u.shape
    tc = min(LRU_CHUNK, L)
    n = L // tc
    blk_f = pl.BlockSpec((None, tc, C), lambda b, j: (b, j, 0))
    blk_b = pl.BlockSpec((None, tc, C), lambda b, j: (b, n - 1 - j, 0))
    state = pl.BlockSpec((2, None, 1, C), lambda b, j: (0, b, 0, 0))
    return pl.pallas_call(
        _lru_kernel,
        grid=(B, n),
        in_specs=[blk_f, blk_b, _full(conv_w.shape), _full(conv_b.shape), _full(wg.shape),
                  _full(ba.shape), _full(bx.shape), _full(lam.shape), state],
        out_specs=[blk_f, blk_b, state],
        out_shape=[jax.ShapeDtypeStruct((B, L, C), out_dtype),
                   jax.ShapeDtypeStruct((B, L, C), out_dtype),
                   jax.ShapeDtypeStruct((2, B, 1, C), F32)],
        scratch_shapes=[pltpu.VMEM((tc + SUBLANES, C), F32), pltpu.VMEM((tc + SUBLANES, C), F32),
                        pltpu.VMEM((tc, C), F32), pltpu.VMEM((tc, C), F32), pltpu.VMEM((tc, C), F32),
                        pltpu.VMEM((1, C), F32), pltpu.VMEM((1, C), F32)],
        compiler_params=_cparams(("arbitrary", "arbitrary")),
        name="lru",
    )(u, u, conv_w, conv_b, wg, ba, bx, lam, h0)


def _gate_weights(wa, wx):
    per = MXU_W // LRU_BLOCK_W
    nd, nb, k, _ = wa.shape
    eye = jnp.eye(per, dtype=wa.dtype)

    def bd(w):
        w = w.reshape(nd, nb // per, per, k, k)
        return jnp.einsum('dgpkj,pq->dgpkqj', w, eye).reshape(nd, nb // per, MXU_W, MXU_W)

    return jnp.concatenate([bd(wa), bd(wx)], axis=-1).astype(BF16)


def _merge_kernel(gb_ref, gc_ref, xc_ref, gcp_ref, xcp_ref, gcn_ref, xcn_ref,
                  lg_ref, hf_ref, hb_ref, ma_ref, mb_ref, x_ref,
                  cw_ref, woa_ref, wob_ref, wo_ref, bm_ref, gate_ref, nf_ref, sh_ref, sc_ref,
                  rwt_ref, rb_ref,
                  x1_ref, fx_ref, idx_ref, gts_ref):
    i = pl.program_id(1)
    n = pl.num_programs(1)
    tm, C = gc_ref.shape
    half = C // 2
    E = rwt_ref.shape[0]

    u = gc_ref[...].astype(F32) * xc_ref[...].astype(F32)
    uh = u[:, :half]
    col = lax.broadcasted_iota(jnp.int32, (tm, half), 0) % GRID_W
    left = jnp.where(col >= 1, pltpu.roll(uh, 1, 0), 0.0)
    right = jnp.where(col <= GRID_W - 2, pltpu.roll(uh, tm - 1, 0), 0.0)
    yh = (cw_ref[0:1, :half] * left + cw_ref[1:2, :half] * uh + cw_ref[2:3, :half] * right)
    uv = u[:, half:]
    prev = gcp_ref[...].astype(F32) * xcp_ref[...].astype(F32)
    nxt = gcn_ref[...].astype(F32) * xcn_ref[...].astype(F32)
    prev = jnp.where(i > 0, prev, 0.0)
    nxt = jnp.where(i < n - 1, nxt, 0.0)
    up = jnp.concatenate([prev, uv[:tm - GRID_W]], axis=0)
    down = jnp.concatenate([uv[GRID_W:], nxt], axis=0)
    yv = (cw_ref[0:1, half:] * up + cw_ref[1:2, half:] * uv + cw_ref[2:3, half:] * down)

    gb = gb_ref[...].astype(F32)
    za_h = (gb[:, :half] * yh).astype(BF16)
    za_v = (gb[:, half:] * yv).astype(BF16)
    y_a = (jnp.dot(za_h, woa_ref[:half, :], preferred_element_type=F32)
           + jnp.dot(za_v, woa_ref[half:, :], preferred_element_type=F32))

    h_lru = hf_ref[...].astype(F32) + hb_ref[...].astype(F32)
    zb = (jax.nn.gelu(lg_ref[...].astype(F32), approximate=True) * h_lru).astype(BF16)
    y_b = jnp.dot(zb, wob_ref[...], preferred_element_type=F32)

    merged = (jax.nn.sigmoid(ma_ref[...].astype(F32) + bm_ref[0:1, :]) * y_a
              + jax.nn.sigmoid(mb_ref[...].astype(F32) + bm_ref[1:2, :]) * y_b)
    mix = jnp.dot(merged.astype(BF16), wo_ref[...], preferred_element_type=F32)
    x1 = x_ref[...] + gate_ref[...] * mix
    x1_ref[...] = x1

    ms = jnp.mean(x1 * x1, axis=-1, keepdims=True)
    fx = (x1 * lax.rsqrt(ms + EPS) * nf_ref[...]) * (1.0 + sc_ref[...]) + sh_ref[...]
    fx_ref[...] = fx

    logits = lax.dot_general(rwt_ref[...], fx, (((1,), (1,)), ((), ())),
                             preferred_element_type=F32,
                             precision=lax.Precision.HIGHEST) + rb_ref[...]
    erow = lax.broadcasted_iota(jnp.int32, (E, tm), 0).astype(F32)
    vals = []
    for k in range(TOP_K):
        m = jnp.max(logits, axis=0, keepdims=True)
        am = jnp.min(jnp.where(logits == m, erow, float(E)), axis=0, keepdims=True)
        vals.append(m)
        idx_ref[k:k + 1, :] = am.astype(jnp.int32)
        logits = jnp.where(erow == am, -jnp.inf, logits)
    ex = [jnp.exp(v - vals[0]) for v in vals]
    den = ex[0] + ex[1] + ex[2] + ex[3]
    for k in range(TOP_K):
        gts_ref[k:k + 1, :] = ex[k] / den


def merge(cols, hf, hb, x, conv_w, w_out_a, w_out_b, w_o, b_merge, gate, norm_ffn, shift, scale,
          router_wt, router_b):
    gb, gc, xc, _, lg, ma, mb = cols
    B, S, D = x.shape
    C = gc.shape[-1]
    half = C // 2
    E = router_wt.shape[0]
    tm = min(TOKEN_TILE, S)
    n = S // tm
    r = tm // GRID_W
    n_rows = S // GRID_W
    tile = lambda w: pl.BlockSpec((None, tm, w), lambda b, i: (b, i, 0))
    prev = pl.BlockSpec((None, GRID_W, half), lambda b, i: (b, jnp.maximum(i * r - 1, 0), 1))
    nxt = pl.BlockSpec((None, GRID_W, half),
                       lambda b, i: (b, jnp.minimum((i + 1) * r, n_rows - 1), 1))
    vec = pl.BlockSpec((None, 1, D), lambda b, i: (b, 0, 0))
    lane_out = pl.BlockSpec((None, TOP_K, tm), lambda b, i: (b, 0, i))
    return pl.pallas_call(
        _merge_kernel,
        grid=(B, n),
        in_specs=[tile(C), tile(C), tile(C), prev, prev, nxt, nxt,
                  tile(C), tile(C), tile(C), tile(D), tile(D), tile(D),
                  _full(conv_w.shape), _full(w_out_a.shape), _full(w_out_b.shape),
                  _full(w_o.shape), _full(b_merge.shape), vec, _full((1, D)), vec, vec,
                  _full(router_wt.shape), _full((E, 1))],
        out_specs=[tile(D), tile(D), lane_out, lane_out],
        out_shape=[jax.ShapeDtypeStruct((B, S, D), F32), jax.ShapeDtypeStruct((B, S, D), F32),
                   jax.ShapeDtypeStruct((B, TOP_K, S), jnp.int32),
                   jax.ShapeDtypeStruct((B, TOP_K, S), F32)],
        compiler_params=_cparams(("arbitrary", "arbitrary")),
        name="merge",
    )(gb, gc, xc, gc, xc, gc, xc, lg, hf, hb, ma, mb, x,
      conv_w, w_out_a, w_out_b, w_o, b_merge, gate, norm_ffn.reshape(1, D), shift, scale,
      router_wt, router_b.reshape(E, 1))


def _selection(idx_ref, n_experts):
    tp = idx_ref.shape[1]
    idx = idx_ref[...]
    erow = lax.broadcasted_iota(jnp.int32, (n_experts, tp), 0)
    sel = [erow == idx[k:k + 1, :] for k in range(TOP_K)]
    onehot = sum(s.astype(F32) for s in sel)
    tile_cnt = jnp.sum(onehot, axis=1, keepdims=True).astype(jnp.int32)
    return sel, onehot, tile_cnt


def _count_kernel(idx_ref, cnt_ref):
    first = (pl.program_id(0) == 0) & (pl.program_id(1) == 0)
    _, _, tile_cnt = _selection(idx_ref, cnt_ref.shape[0])

    @pl.when(first)
    def _():
        cnt_ref[...] = jnp.zeros_like(cnt_ref)

    cnt_ref[...] += tile_cnt


def _plan_kernel(idx_ref, cnt_ref, dest_ref, blk_ref, base_s, tri_s):
    first = (pl.program_id(0) == 0) & (pl.program_id(1) == 0)
    tp = idx_ref.shape[1]
    E = cnt_ref.shape[0]
    n_blk = blk_ref.shape[1]
    shift = MOE_BLOCK.bit_length() - 1
    sel, onehot, tile_cnt = _selection(idx_ref, E)

    @pl.when(first)
    def _():
        cnt = cnt_ref[...]
        padded = ((cnt + (MOE_BLOCK - 1)) >> shift) << shift
        r_i = lax.broadcasted_iota(jnp.int32, (E, E), 0)
        c_i = lax.broadcasted_iota(jnp.int32, (E, E), 1)
        padded_row = jnp.sum(jnp.where(r_i == c_i, padded, 0).astype(F32), axis=0,
                             keepdims=True)
        pstart = jnp.sum(jnp.where(c_i < r_i, padded_row, 0.0), axis=1,
                         keepdims=True).astype(jnp.int32)
        base_s[...] = pstart
        bs = lax.broadcasted_iota(jnp.int32, (E, n_blk), 1) * MOE_BLOCK
        inb = (bs >= pstart) & (bs < pstart + padded)
        e_i = lax.broadcasted_iota(jnp.int32, (E, n_blk), 0)
        valid = jnp.clip(cnt - (bs - pstart), 0, MOE_BLOCK)
        blk_ref[0:1, :] = jnp.sum(jnp.where(inb, e_i, 0).astype(F32), axis=0,
                                  keepdims=True).astype(jnp.int32)
        blk_ref[1:2, :] = jnp.sum(jnp.where(inb, valid, 0).astype(F32), axis=0,
                                  keepdims=True).astype(jnp.int32)
        t_r = lax.broadcasted_iota(jnp.int32, (tp, tp), 0)
        t_c = lax.broadcasted_iota(jnp.int32, (tp, tp), 1)
        tri_s[...] = (t_r < t_c).astype(BF16)

    pref = jnp.dot(onehot.astype(BF16), tri_s[...], preferred_element_type=F32)
    tot = pref + base_s[...].astype(F32)
    for k in range(TOP_K):
        dest_ref[k:k + 1, :] = jnp.sum(jnp.where(sel[k], tot, 0.0), axis=0,
                                       keepdims=True).astype(jnp.int32)
    base_s[...] += tile_cnt


def plan(idx, n_experts, n_blk):
    B, K, S = idx.shape
    tp = min(TOKEN_TILE, S)
    lane = pl.BlockSpec((None, K, tp), lambda b, i: (b, 0, i))
    cnt = pl.pallas_call(
        _count_kernel,
        grid=(B, S // tp),
        in_specs=[lane],
        out_specs=_full((n_experts, 1)),
        out_shape=jax.ShapeDtypeStruct((n_experts, 1), jnp.int32),
        compiler_params=_cparams(("arbitrary", "arbitrary")),
        name="count",
    )(idx)
    return pl.pallas_call(
        _plan_kernel,
        grid=(B, S // tp),
        in_specs=[lane, _full((n_experts, 1))],
        out_specs=[lane, _full((2, n_blk))],
        out_shape=[jax.ShapeDtypeStruct((B, K, S), jnp.int32),
                   jax.ShapeDtypeStruct((2, n_blk), jnp.int32)],
        scratch_shapes=[pltpu.VMEM((n_experts, 1), jnp.int32), pltpu.VMEM((tp, tp), BF16)],
        compiler_params=_cparams(("arbitrary", "arbitrary")),
        name="plan",
    )(idx, cnt)


def _row_copy(src, s, dst, d, sem):
    return pltpu.make_async_copy(src.at[pl.ds(s, 1), :], dst.at[pl.ds(d, 1), :], sem)


def _dispatch_kernel(dest_ref, fx_ref, xs_hbm, sem):
    tq = dest_ref.shape[1]

    def issue(t, c):
        for k in range(TOP_K):
            _row_copy(fx_ref, t, xs_hbm, dest_ref[k, t], sem).start()
        return c

    lax.fori_loop(0, tq, issue, 0)

    def drain(t, c):
        for k in range(TOP_K):
            _row_copy(fx_ref, t, xs_hbm, dest_ref[k, t], sem).wait()
        return c

    lax.fori_loop(0, tq, drain, 0)


def dispatch(dest, fx, n_rows):
    B, K, S = dest.shape
    D = fx.shape[-1]
    tq = min(TOKEN_TILE, S)
    n = S // tq
    return pl.pallas_call(
        _dispatch_kernel,
        grid=(B, n),
        in_specs=[pl.BlockSpec((None, K, tq), lambda b, i: (b, 0, i), memory_space=pltpu.SMEM),
                  pl.BlockSpec((tq, D), lambda b, i: (b * n + i, 0))],
        out_specs=pl.BlockSpec(memory_space=pl.ANY),
        out_shape=jax.ShapeDtypeStruct((n_rows, D), fx.dtype),
        scratch_shapes=[pltpu.SemaphoreType.DMA(())],
        compiler_params=_cparams(("arbitrary", "arbitrary")),
        name="dispatch",
    )(dest, fx.reshape(B * S, D))


def _expert_kernel(blk_ref, x_ref, w1g_ref, w1l_ref, b1g_ref, b1l_ref, w2_ref, b2_ref, y_ref):
    j = pl.program_id(0)
    nvalid = blk_ref[1, j]

    @pl.when(nvalid > 0)
    def _():
        rows = lax.broadcasted_iota(jnp.int32, x_ref.shape, 0)
        x = jnp.where(rows < nvalid, x_ref[...], 0.0).astype(BF16)
        nt = (((1,), (1,)), ((), ()))
        hg = lax.dot_general(x, w1g_ref[...], nt, preferred_element_type=F32) + b1g_ref[...]
        hl = lax.dot_general(x, w1l_ref[...], nt, preferred_element_type=F32) + b1l_ref[...]
        glu = jnp.minimum(hg, SWIGLU_LIMIT)
        lin = jnp.clip(hl, -SWIGLU_LIMIT, SWIGLU_LIMIT)
        act = glu * jax.nn.sigmoid(SWIGLU_ALPHA * glu) * (lin + 1.0)
        y_ref[...] = jnp.dot(act.astype(BF16), w2_ref[...], preferred_element_type=F32) + b2_ref[...]

    @pl.when(nvalid == 0)
    def _():
        y_ref[...] = jnp.zeros_like(y_ref)


def experts(blk, xs, w1g, w1l, b1g, b1l, w2, b2):
    n_rows, D = xs.shape
    E, F, _ = w1g.shape
    n_blk = n_rows // MOE_BLOCK
    rows = pl.BlockSpec((MOE_BLOCK, D), lambda j, blk: (j, 0))
    wsel = lambda a, b: pl.BlockSpec((None, a, b), lambda j, blk: (blk[0, j], 0, 0))
    return pl.pallas_call(
        _expert_kernel,
        grid_spec=pltpu.PrefetchScalarGridSpec(
            num_scalar_prefetch=1,
            grid=(n_blk,),
            in_specs=[rows, wsel(F, D), wsel(F, D), wsel(1, F), wsel(1, F), wsel(F, D), wsel(1, D)],
            out_specs=rows),
        out_shape=jax.ShapeDtypeStruct((n_rows, D), F32),
        compiler_params=_cparams(("arbitrary",)),
        name="experts",
    )(blk, xs, w1g, w1l, b1g, b1l, w2, b2)


def _combine_kernel(dest_ref, ys_hbm, gts_ref, x1_ref, gate_ref, nf_ref, o_ref, buf, sem):
    tq = x1_ref.shape[0]

    def issue(t, c):
        for k in range(TOP_K):
            pltpu.make_async_copy(ys_hbm.at[pl.ds(dest_ref[k, t], 1), :],
                                  buf.at[k, pl.ds(t, 1), :], sem).start()
        return c

    lax.fori_loop(0, tq, issue, 0)

    def drain(t, c):
        for k in range(TOP_K):
            pltpu.make_async_copy(ys_hbm.at[pl.ds(dest_ref[k, t], 1), :],
                                  buf.at[k, pl.ds(t, 1), :], sem).wait()
        return c

    lax.fori_loop(0, tq, drain, 0)

    g = gts_ref[...]
    gt = jnp.concatenate([g, jnp.zeros_like(g)], axis=0).T
    y = gt[:, 0:1] * buf[0]
    for k in range(1, TOP_K):
        y = y + gt[:, k:k + 1] * buf[k]
    xo = x1_ref[...] + gate_ref[...] * y
    ms = jnp.mean(xo * xo, axis=-1, keepdims=True)
    o_ref[...] = xo * lax.rsqrt(ms + EPS) * nf_ref[...]


def combine(dest, ys, gts, x1, gate, norm_final):
    B, S, D = x1.shape
    K = dest.shape[1]
    tq = min(TOKEN_TILE, S)
    lane = lambda ms: pl.BlockSpec((None, K, tq), lambda b, i: (b, 0, i), memory_space=ms)
    tile = pl.BlockSpec((None, tq, D), lambda b, i: (b, i, 0))
    return pl.pallas_call(
        _combine_kernel,
        grid=(B, S // tq),
        in_specs=[lane(pltpu.SMEM), pl.BlockSpec(memory_space=pl.ANY), lane(pltpu.VMEM), tile,
                  pl.BlockSpec((None, 1, D), lambda b, i: (b, 0, 0)), _full((1, D))],
        out_specs=tile,
        out_shape=jax.ShapeDtypeStruct((B, S, D), F32),
        scratch_shapes=[pltpu.VMEM((K, tq, D), F32), pltpu.SemaphoreType.DMA(())],
        compiler_params=_cparams(("arbitrary", "arbitrary")),
        name="combine",
    )(dest, ys, gts, x1, gate, norm_final.reshape(1, D))


def kernel(x, c, ctx, c_ctx, w_ada, b_ada, norm_mix, w_in, conv_a_w, w_out_a, lru_conv_w,
           lru_conv_b, lru_wa, lru_ba, lru_wx, lru_bx, lru_lambda, w_out_b, b_merge, w_o,
           norm_ffn, router_w, router_b, w1, b1, w2, b2, norm_final):
    B, S, D = x.shape
    depth = w_ada.shape[0]
    assert depth == 1, "single-layer block"
    l = 0
    E = router_w.shape[-1]
    n_groups = w_in.shape[-1] // D
    lru_col = 3

    pad = -(B + 1) % SUBLANES
    cond = jnp.concatenate([c, c_ctx[None], jnp.zeros((pad, D), F32)], axis=0)
    mod = adaln(cond, w_ada[l], b_ada[l])
    mx = [mod[:B, i * D:(i + 1) * D].reshape(B, 1, D) for i in range(N_MOD)]
    mc = [jnp.broadcast_to(mod[B:B + 1, i * D:(i + 1) * D].reshape(1, 1, D), (B, 1, D))
          for i in range(2)]

    w_in_b = w_in[l].astype(BF16)
    wg = _gate_weights(lru_wa[l], lru_wx[l])
    lru_args = (lru_conv_w[l], lru_conv_b[l], wg, lru_ba[l], lru_bx[l], lru_lambda[l])

    (cs_lru,) = inproj(ctx, norm_mix[l], mc[0], mc[1], w_in_b[:, lru_col * D:(lru_col + 1) * D], 1)
    h_zero = jnp.zeros((2, B, 1, D), F32)
    _, _, h_ctx = lru(cs_lru, *lru_args, h_zero, BF16)

    cols = inproj(x, norm_mix[l], mx[0], mx[1], w_in_b, n_groups)
    hf, hb, _ = lru(cols[lru_col], *lru_args, h_ctx, BF16)

    x1, fx, idx, gts = merge(cols, hf, hb, x, conv_a_w[l], w_out_a[l].astype(BF16),
                             w_out_b[l].astype(BF16), w_o[l].astype(BF16), b_merge[l], mx[2],
                             norm_ffn[l], mx[3], mx[4], router_w[l].T, router_b[l])

    T = B * S
    n_blk = (T * TOP_K + MOE_BLOCK - 1) // MOE_BLOCK + E
    dest, blk = plan(idx, E, n_blk)
    xs = dispatch(dest, fx, n_blk * MOE_BLOCK)
    w1t = jnp.swapaxes(w1[l], 1, 2)
    ys = experts(blk, xs, w1t[:, 0::2, :].astype(BF16), w1t[:, 1::2, :].astype(BF16),
                 b1[l][:, None, 0::2], b1[l][:, None, 1::2], w2[l].astype(BF16), b2[l][:, None, :])
    return combine(dest, ys, gts, x1, mx[5], norm_final)
```

```python
import functools

import jax
import jax.numpy as jnp
from jax import lax
from jax.experimental import pallas as pl
from jax.experimental.pallas import tpu as pltpu

F32 = jnp.float32
BF16 = jnp.bfloat16

EPS = 1e-6
GRID_W = 64
CONV_W = 3
LRU_CONV_W = 4
LRU_BLOCK_W = 64
LRU_C = 8.0
N_MOD = 6
TOP_K = 4
MOE_BLOCK = 512
SWIGLU_LIMIT = 7.0
SWIGLU_ALPHA = 1.702

MXU_W = 256
SUBLANES = 8
VMEM_LIMIT = 56 * 1024 * 1024
TOKEN_TILE = 512
LRU_CHUNK = 512


def _cparams(sem):
    return pltpu.CompilerParams(dimension_semantics=sem, vmem_limit_bytes=VMEM_LIMIT)


def _full(shape):
    nd = len(shape)
    return pl.BlockSpec(shape, lambda *_: (0,) * nd)


def _adaln_kernel(cond_ref, w_ref, b_ref, o_ref):
    c = cond_ref[...]
    s = c * jax.nn.sigmoid(c)
    o_ref[...] = jnp.dot(s, w_ref[...], preferred_element_type=F32,
                         precision=lax.Precision.HIGHEST) + b_ref[...]


def adaln(cond, w, b):
    R, D = cond.shape
    N = w.shape[1]
    tn = D
    return pl.pallas_call(
        _adaln_kernel,
        grid=(N // tn,),
        in_specs=[pl.BlockSpec((R, D), lambda j: (0, 0)),
                  pl.BlockSpec((D, tn), lambda j: (0, j)),
                  pl.BlockSpec((1, tn), lambda j: (0, j))],
        out_specs=pl.BlockSpec((R, tn), lambda j: (0, j)),
        out_shape=jax.ShapeDtypeStruct((R, N), F32),
        compiler_params=_cparams(("arbitrary",)),
        name="adaln",
    )(cond, w, b.reshape(1, N))


def _inproj_kernel(x_ref, g_ref, shift_ref, scale_ref, w_ref, *out_refs):
    xf = x_ref[...]
    ms = jnp.mean(xf * xf, axis=-1, keepdims=True)
    y = xf * lax.rsqrt(ms + EPS) * g_ref[...]
    h = (y * (1.0 + scale_ref[...]) + shift_ref[...]).astype(BF16)
    for j, o in enumerate(out_refs):
        n = o.shape[-1]
        o[...] = jnp.dot(h, w_ref[:, j * n:(j + 1) * n],
                         preferred_element_type=F32).astype(o.dtype)


def inproj(x, g, shift, scale, w, n_groups):
    B, L, D = x.shape
    tm = min(TOKEN_TILE, L)
    n = w.shape[1] // n_groups
    tile = pl.BlockSpec((None, tm, D), lambda b, i: (b, i, 0))
    vec = pl.BlockSpec((None, 1, D), lambda b, i: (b, 0, 0))
    return pl.pallas_call(
        _inproj_kernel,
        grid=(B, L // tm),
        in_specs=[tile, _full((1, D)), vec, vec,
                  pl.BlockSpec(w.shape, lambda b, i: (0, 0), pipeline_mode=pl.Buffered(1))],
        out_specs=[pl.BlockSpec((None, tm, n), lambda b, i: (b, i, 0))] * n_groups,
        out_shape=[jax.ShapeDtypeStruct((B, L, n), BF16)] * n_groups,
        compiler_params=_cparams(("arbitrary", "arbitrary")),
        name="inproj",
    )(x, g.reshape(1, D), shift, scale, w)


def _softplus(z):
    return jnp.maximum(z, 0.0) + jnp.log1p(jnp.exp(-jnp.abs(z)))


def _lru_kernel(uf_ref, ub_ref, cw_ref, cb_ref, wg_ref, ba_ref, bx_ref, lam_ref, h0_ref,
                hf_ref, hb_ref, hlast_ref,
                uext_f, uext_b, a_s, b_s, h_s, carry_f, carry_b):
    j = pl.program_id(1)
    tc, C = uf_ref.shape
    halo = SUBLANES
    n_groups = C // MXU_W
    nblk = tc // SUBLANES

    @pl.when(j == 0)
    def _():
        uext_f[0:halo, :] = jnp.zeros((halo, C), F32)
        uext_b[tc:tc + halo, :] = jnp.zeros((halo, C), F32)
        carry_f[...] = h0_ref[0]
        carry_b[...] = h0_ref[1]

    row = lax.broadcasted_iota(jnp.int32, (SUBLANES, C), 0)

    def one_dir(d, u_ref, uext, o_ref, carry, reverse):
        if reverse:
            uext[0:tc, :] = u_ref[...].astype(F32)
            offs = [LRU_CONV_W - 1 - q for q in range(LRU_CONV_W)]
        else:
            uext[halo:halo + tc, :] = u_ref[...].astype(F32)
            offs = [halo - (LRU_CONV_W - 1) + q for q in range(LRU_CONV_W)]
        v = cb_ref[pl.ds(d, 1), :]
        for q in range(LRU_CONV_W):
            v = v + cw_ref[d, pl.ds(q, 1), :] * uext[pl.ds(offs[q], tc), :]
        if reverse:
            uext[tc:tc + halo, :] = uext[0:halo, :]
        else:
            uext[0:halo, :] = uext[tc:tc + halo, :]

        c8 = -LRU_C * _softplus(-lam_ref[pl.ds(d, 1), :])
        vb = v.astype(BF16)
        for g in range(n_groups):
            cols = slice(g * MXU_W, (g + 1) * MXU_W)
            rg = jnp.dot(vb[:, cols], wg_ref[d, g], preferred_element_type=F32)
            r = jax.nn.sigmoid(rg[:, :MXU_W] + ba_ref[pl.ds(d, 1), cols])
            i = jax.nn.sigmoid(rg[:, MXU_W:] + bx_ref[pl.ds(d, 1), cols])
            log_a = c8[:, cols] * r
            a = jnp.exp(log_a)
            a_s[:, cols] = a
            b_s[:, cols] = jnp.sqrt(-jnp.tanh(log_a) * (a * a + 1.0)) * (i * v[:, cols])

        def body(k, h):
            kk = (nblk - 1 - k) if reverse else k
            sl = pl.ds(pl.multiple_of(kk * SUBLANES, SUBLANES), SUBLANES)
            a = a_s[sl, :]
            b = b_s[sl, :]
            for s in (1, 2, 4):
                if reverse:
                    m = row < SUBLANES - s
                    sh = SUBLANES - s
                else:
                    m = row >= s
                    sh = s
                a_sh = jnp.where(m, pltpu.roll(a, sh, 0), 1.0)
                b_sh = jnp.where(m, pltpu.roll(b, sh, 0), 0.0)
                b = a * b_sh + b
                a = a * a_sh
            hblk = a * h + b
            h_s[sl, :] = hblk
            return hblk[0:1, :] if reverse else hblk[SUBLANES - 1:SUBLANES, :]

        h = lax.fori_loop(0, nblk, body, carry[...])
        carry[...] = h
        o_ref[...] = h_s[...].astype(o_ref.dtype)

    one_dir(0, uf_ref, uext_f, hf_ref, carry_f, False)
    one_dir(1, ub_ref, uext_b, hb_ref, carry_b, True)
    hlast_ref[0] = carry_f[...]
    hlast_ref[1] = carry_b[...]


def lru(u, conv_w, conv_b, wg, ba, bx, lam, h0, out_dtype):
    B, L, C = u.shape
    tc = min(LRU_CHUNK, L)
    n = L // tc
    blk_f = pl.BlockSpec((None, tc, C), lambda b, j: (b, j, 0))
    blk_b = pl.BlockSpec((None, tc, C), lambda b, j: (b, n - 1 - j, 0))
    state = pl.BlockSpec((2, None, 1, C), lambda b, j: (0, b, 0, 0))
    return pl.pallas_call(
        _lru_kernel,
        grid=(B, n),
        in_specs=[blk_f, blk_b, _full(conv_w.shape), _full(conv_b.shape), _full(wg.shape),
                  _full(ba.shape), _full(bx.shape), _full(lam.shape), state],
        out_specs=[blk_f, blk_b, state],
        out_shape=[jax.ShapeDtypeStruct((B, L, C), out_dtype),
                   jax.ShapeDtypeStruct((B, L, C), out_dtype),
                   jax.ShapeDtypeStruct((2, B, 1, C), F32)],
        scratch_shapes=[pltpu.VMEM((tc + SUBLANES, C), F32), pltpu.VMEM((tc + SUBLANES, C), F32),
                        pltpu.VMEM((tc, C), F32), pltpu.VMEM((tc, C), F32), pltpu.VMEM((tc, C), F32),
                        pltpu.VMEM((1, C), F32), pltpu.VMEM((1, C), F32)],
        compiler_params=_cparams(("arbitrary", "arbitrary")),
        name="lru",
    )(u, u, conv_w, conv_b, wg, ba, bx, lam, h0)


def _gate_weights(wa, wx):
    per = MXU_W // LRU_BLOCK_W
    nd, nb, k, _ = wa.shape
    eye = jnp.eye(per, dtype=wa.dtype)

    def bd(w):
        w = w.reshape(nd, nb // per, per, k, k)
        return jnp.einsum('dgpkj,pq->dgpkqj', w, eye).reshape(nd, nb // per, MXU_W, MXU_W)

    return jnp.concatenate([bd(wa), bd(wx)], axis=-1).astype(BF16)


def _merge_kernel(gb_ref, gc_ref, xc_ref, gcp_ref, xcp_ref, gcn_ref, xcn_ref,
                  lg_ref, hf_ref, hb_ref, ma_ref, mb_ref, x_ref,
                  cw_ref, woa_ref, wob_ref, wo_ref, bm_ref, gate_ref, nf_ref, sh_ref, sc_ref,
                  rwt_ref, rb_ref,
                  x1_ref, fx_ref, idx_ref, gts_ref):
    i = pl.program_id(1)
    n = pl.num_programs(1)
    tm, C = gc_ref.shape
    half = C // 2
    E = rwt_ref.shape[0]

    u = gc_ref[...].astype(F32) * xc_ref[...].astype(F32)
    uh = u[:, :half]
    col = lax.broadcasted_iota(jnp.int32, (tm, half), 0) % GRID_W
    left = jnp.where(col >= 1, pltpu.roll(uh, 1, 0), 0.0)
    right = jnp.where(col <= GRID_W - 2, pltpu.roll(uh, tm - 1, 0), 0.0)
    yh = (cw_ref[0:1, :half] * left + cw_ref[1:2, :half] * uh + cw_ref[2:3, :half] * right)
    uv = u[:, half:]
    prev = gcp_ref[...].astype(F32) * xcp_ref[...].astype(F32)
    nxt = gcn_ref[...].astype(F32) * xcn_ref[...].astype(F32)
    prev = jnp.where(i > 0, prev, 0.0)
    nxt = jnp.where(i < n - 1, nxt, 0.0)
    up = jnp.concatenate([prev, uv[:tm - GRID_W]], axis=0)
    down = jnp.concatenate([uv[GRID_W:], nxt], axis=0)
    yv = (cw_ref[0:1, half:] * up + cw_ref[1:2, half:] * uv + cw_ref[2:3, half:] * down)

    gb = gb_ref[...].astype(F32)
    za_h = (gb[:, :half] * yh).astype(BF16)
    za_v = (gb[:, half:] * yv).astype(BF16)
    y_a = (jnp.dot(za_h, woa_ref[:half, :], preferred_element_type=F32)
           + jnp.dot(za_v, woa_ref[half:, :], preferred_element_type=F32))

    h_lru = hf_ref[...].astype(F32) + hb_ref[...].astype(F32)
    zb = (jax.nn.gelu(lg_ref[...].astype(F32), approximate=True) * h_lru).astype(BF16)
    y_b = jnp.dot(zb, wob_ref[...], preferred_element_type=F32)

    merged = (jax.nn.sigmoid(ma_ref[...].astype(F32) + bm_ref[0:1, :]) * y_a
              + jax.nn.sigmoid(mb_ref[...].astype(F32) + bm_ref[1:2, :]) * y_b)
    mix = jnp.dot(merged.astype(BF16), wo_ref[...], preferred_element_type=F32)
    x1 = x_ref[...] + gate_ref[...] * mix
    x1_ref[...] = x1

    ms = jnp.mean(x1 * x1, axis=-1, keepdims=True)
    fx = (x1 * lax.rsqrt(ms + EPS) * nf_ref[...]) * (1.0 + sc_ref[...]) + sh_ref[...]
    fx_ref[...] = fx

    logits = lax.dot_general(rwt_ref[...], fx, (((1,), (1,)), ((), ())),
                             preferred_element_type=F32,
                             precision=lax.Precision.HIGHEST) + rb_ref[...]
    erow = lax.broadcasted_iota(jnp.int32, (E, tm), 0).astype(F32)
    vals = []
    for k in range(TOP_K):
        m = jnp.max(logits, axis=0, keepdims=True)
        am = jnp.min(jnp.where(logits == m, erow, float(E)), axis=0, keepdims=True)
        vals.append(m)
        idx_ref[k:k + 1, :] = am.astype(jnp.int32)
        logits = jnp.where(erow == am, -jnp.inf, logits)
    ex = [jnp.exp(v - vals[0]) for v in vals]
    den = ex[0] + ex[1] + ex[2] + ex[3]
    for k in range(TOP_K):
        gts_ref[k:k + 1, :] = ex[k] / den


def merge(cols, hf, hb, x, conv_w, w_out_a, w_out_b, w_o, b_merge, gate, norm_ffn, shift, scale,
          router_wt, router_b):
    gb, gc, xc, _, lg, ma, mb = cols
    B, S, D = x.shape
    C = gc.shape[-1]
    half = C // 2
    E = router_wt.shape[0]
    tm = min(TOKEN_TILE, S)
    n = S // tm
    r = tm // GRID_W
    n_rows = S // GRID_W
    tile = lambda w: pl.BlockSpec((None, tm, w), lambda b, i: (b, i, 0))
    prev = pl.BlockSpec((None, GRID_W, half), lambda b, i: (b, jnp.maximum(i * r - 1, 0), 1))
    nxt = pl.BlockSpec((None, GRID_W, half),
                       lambda b, i: (b, jnp.minimum((i + 1) * r, n_rows - 1), 1))
    vec = pl.BlockSpec((None, 1, D), lambda b, i: (b, 0, 0))
    lane_out = pl.BlockSpec((None, TOP_K, tm), lambda b, i: (b, 0, i))
    return pl.pallas_call(
        _merge_kernel,
        grid=(B, n),
        in_specs=[tile(C), tile(C), tile(C), prev, prev, nxt, nxt,
                  tile(C), tile(C), tile(C), tile(D), tile(D), tile(D),
                  _full(conv_w.shape), _full(w_out_a.shape), _full(w_out_b.shape),
                  _full(w_o.shape), _full(b_merge.shape), vec, _full((1, D)), vec, vec,
                  _full(router_wt.shape), _full((E, 1))],
        out_specs=[tile(D), tile(D), lane_out, lane_out],
        out_shape=[jax.ShapeDtypeStruct((B, S, D), F32), jax.ShapeDtypeStruct((B, S, D), F32),
                   jax.ShapeDtypeStruct((B, TOP_K, S), jnp.int32),
                   jax.ShapeDtypeStruct((B, TOP_K, S), F32)],
        compiler_params=_cparams(("arbitrary", "arbitrary")),
        name="merge",
    )(gb, gc, xc, gc, xc, gc, xc, lg, hf, hb, ma, mb, x,
      conv_w, w_out_a, w_out_b, w_o, b_merge, gate, norm_ffn.reshape(1, D), shift, scale,
      router_wt, router_b.reshape(E, 1))


def _selection(idx_ref, n_experts):
    tp = idx_ref.shape[1]
    idx = idx_ref[...]
    erow = lax.broadcasted_iota(jnp.int32, (n_experts, tp), 0)
    sel = [erow == idx[k:k + 1, :] for k in range(TOP_K)]
    onehot = sum(s.astype(F32) for s in sel)
    tile_cnt = jnp.sum(onehot, axis=1, keepdims=True).astype(jnp.int32)
    return sel, onehot, tile_cnt


def _count_kernel(idx_ref, cnt_ref):
    first = (pl.program_id(0) == 0) & (pl.program_id(1) == 0)
    _, _, tile_cnt = _selection(idx_ref, cnt_ref.shape[0])

    @pl.when(first)
    def _():
        cnt_ref[...] = jnp.zeros_like(cnt_ref)

    cnt_ref[...] += tile_cnt


def _plan_kernel(idx_ref, cnt_ref, dest_ref, blk_ref, base_s, tri_s):
    first = (pl.program_id(0) == 0) & (pl.program_id(1) == 0)
    tp = idx_ref.shape[1]
    E = cnt_ref.shape[0]
    n_blk = blk_ref.shape[1]
    shift = MOE_BLOCK.bit_length() - 1
    sel, onehot, tile_cnt = _selection(idx_ref, E)

    @pl.when(first)
    def _():
        cnt = cnt_ref[...]
        padded = ((cnt + (MOE_BLOCK - 1)) >> shift) << shift
        r_i = lax.broadcasted_iota(jnp.int32, (E, E), 0)
        c_i = lax.broadcasted_iota(jnp.int32, (E, E), 1)
        padded_row = jnp.sum(jnp.where(r_i == c_i, padded, 0).astype(F32), axis=0,
                             keepdims=True)
        pstart = jnp.sum(jnp.where(c_i < r_i, padded_row, 0.0), axis=1,
                         keepdims=True).astype(jnp.int32)
        base_s[...] = pstart
        bs = lax.broadcasted_iota(jnp.int32, (E, n_blk), 1) * MOE_BLOCK
        inb = (bs >= pstart) & (bs < pstart + padded)
        e_i = lax.broadcasted_iota(jnp.int32, (E, n_blk), 0)
        valid = jnp.clip(cnt - (bs - pstart), 0, MOE_BLOCK)
        blk_ref[0:1, :] = jnp.sum(jnp.where(inb, e_i, 0).astype(F32), axis=0,
                                  keepdims=True).astype(jnp.int32)
        blk_ref[1:2, :] = jnp.sum(jnp.where(inb, valid, 0).astype(F32), axis=0,
                                  keepdims=True).astype(jnp.int32)
        t_r = lax.broadcasted_iota(jnp.int32, (tp, tp), 0)
        t_c = lax.broadcasted_iota(jnp.int32, (tp, tp), 1)
        tri_s[...] = (t_r < t_c).astype(BF16)

    pref = jnp.dot(onehot.astype(BF16), tri_s[...], preferred_element_type=F32)
    tot = pref + base_s[...].astype(F32)
    for k in range(TOP_K):
        dest_ref[k:k + 1, :] = jnp.sum(jnp.where(sel[k], tot, 0.0), axis=0,
                                       keepdims=True).astype(jnp.int32)
    base_s[...] += tile_cnt


def plan(idx, n_experts, n_blk):
    B, K, S = idx.shape
    tp = min(TOKEN_TILE, S)
    lane = pl.BlockSpec((None, K, tp), lambda b, i: (b, 0, i))
    cnt = pl.pallas_call(
        _count_kernel,
        grid=(B, S // tp),
        in_specs=[lane],
        out_specs=_full((n_experts, 1)),
        out_shape=jax.ShapeDtypeStruct((n_experts, 1), jnp.int32),
        compiler_params=_cparams(("arbitrary", "arbitrary")),
        name="count",
    )(idx)
    return pl.pallas_call(
        _plan_kernel,
        grid=(B, S // tp),
        in_specs=[lane, _full((n_experts, 1))],
        out_specs=[lane, _full((2, n_blk))],
        out_shape=[jax.ShapeDtypeStruct((B, K, S), jnp.int32),
                   jax.ShapeDtypeStruct((2, n_blk), jnp.int32)],
        scratch_shapes=[pltpu.VMEM((n_experts, 1), jnp.int32), pltpu.VMEM((tp, tp), BF16)],
        compiler_params=_cparams(("arbitrary", "arbitrary")),
        name="plan",
    )(idx, cnt)


def _row_copy(src, s, dst, d, sem):
    return pltpu.make_async_copy(src.at[pl.ds(s, 1), :], dst.at[pl.ds(d, 1), :], sem)


def _dispatch_kernel(dest_ref, fx_ref, xs_hbm, sem):
    tq = dest_ref.shape[1]

    def issue(t, c):
        for k in range(TOP_K):
            _row_copy(fx_ref, t, xs_hbm, dest_ref[k, t], sem).start()
        return c

    lax.fori_loop(0, tq, issue, 0)

    def drain(t, c):
        for k in range(TOP_K):
            _row_copy(fx_ref, t, xs_hbm, dest_ref[k, t], sem).wait()
        return c

    lax.fori_loop(0, tq, drain, 0)


def dispatch(dest, fx, n_rows):
    B, K, S = dest.shape
    D = fx.shape[-1]
    tq = min(TOKEN_TILE, S)
    n = S // tq
    return pl.pallas_call(
        _dispatch_kernel,
        grid=(B, n),
        in_specs=[pl.BlockSpec((None, K, tq), lambda b, i: (b, 0, i), memory_space=pltpu.SMEM),
                  pl.BlockSpec((tq, D), lambda b, i: (b * n + i, 0))],
        out_specs=pl.BlockSpec(memory_space=pl.ANY),
        out_shape=jax.ShapeDtypeStruct((n_rows, D), fx.dtype),
        scratch_shapes=[pltpu.SemaphoreType.DMA(())],
        compiler_params=_cparams(("arbitrary", "arbitrary")),
        name="dispatch",
    )(dest, fx.reshape(B * S, D))


def _w1prep_kernel(w_ref, g_ref, l_ref):
    half = MXU_W // 2
    r = lax.broadcasted_iota(jnp.int32, (MXU_W, MXU_W), 0)
    c = lax.broadcasted_iota(jnp.int32, (MXU_W, MXU_W), 1)
    perm = (r == jnp.where(c < half, 2 * c, 2 * (c - half) + 1)).astype(BF16)
    for q in range(w_ref.shape[1] // MXU_W):
        blk = w_ref[:, q * MXU_W:(q + 1) * MXU_W].astype(BF16)
        out = jnp.dot(blk, perm, preferred_element_type=F32)
        g_ref[:, q * half:(q + 1) * half] = out[:, :half].astype(BF16)
        l_ref[:, q * half:(q + 1) * half] = out[:, half:].astype(BF16)


def w1prep(w1):
    E, D, F2 = w1.shape
    td = min(TOKEN_TILE, D)
    out = pl.BlockSpec((None, td, F2 // 2), lambda e, i: (e, i, 0))
    return pl.pallas_call(
        _w1prep_kernel,
        grid=(E, D // td),
        in_specs=[pl.BlockSpec((None, td, F2), lambda e, i: (e, i, 0))],
        out_specs=[out, out],
        out_shape=[jax.ShapeDtypeStruct((E, D, F2 // 2), BF16)] * 2,
        compiler_params=_cparams(("arbitrary", "arbitrary")),
        name="w1prep",
    )(w1)


def _expert_kernel(blk_ref, x_ref, w1g_ref, w1l_ref, b1g_ref, b1l_ref, w2_ref, b2_ref, y_ref):
    j = pl.program_id(0)
    nvalid = blk_ref[1, j]

    @pl.when(nvalid > 0)
    def _():
        rows = lax.broadcasted_iota(jnp.int32, x_ref.shape, 0)
        x = jnp.where(rows < nvalid, x_ref[...], 0.0).astype(BF16)
        hg = jnp.dot(x, w1g_ref[...], preferred_element_type=F32) + b1g_ref[...]
        hl = jnp.dot(x, w1l_ref[...], preferred_element_type=F32) + b1l_ref[...]
        glu = jnp.minimum(hg, SWIGLU_LIMIT)
        lin = jnp.clip(hl, -SWIGLU_LIMIT, SWIGLU_LIMIT)
        act = glu * jax.nn.sigmoid(SWIGLU_ALPHA * glu) * (lin + 1.0)
        y_ref[...] = jnp.dot(act.astype(BF16), w2_ref[...], preferred_element_type=F32) + b2_ref[...]

    @pl.when(nvalid == 0)
    def _():
        y_ref[...] = jnp.zeros_like(y_ref)


def experts(blk, xs, w1g, w1l, b1g, b1l, w2, b2):
    n_rows, D = xs.shape
    E, _, F = w1g.shape
    n_blk = n_rows // MOE_BLOCK
    rows = pl.BlockSpec((MOE_BLOCK, D), lambda j, blk: (j, 0))
    wsel = lambda a, b: pl.BlockSpec((None, a, b), lambda j, blk: (blk[0, j], 0, 0))
    return pl.pallas_call(
        _expert_kernel,
        grid_spec=pltpu.PrefetchScalarGridSpec(
            num_scalar_prefetch=1,
            grid=(n_blk,),
            in_specs=[rows, wsel(D, F), wsel(D, F), wsel(1, F), wsel(1, F), wsel(F, D), wsel(1, D)],
            out_specs=rows),
        out_shape=jax.ShapeDtypeStruct((n_rows, D), F32),
        compiler_params=_cparams(("arbitrary",)),
        name="experts",
    )(blk, xs, w1g, w1l, b1g, b1l, w2, b2)


def _combine_kernel(dest_ref, ys_hbm, gts_ref, x1_ref, gate_ref, nf_ref, o_ref, buf, sem):
    tq = x1_ref.shape[0]

    def issue(t, c):
        for k in range(TOP_K):
            pltpu.make_async_copy(ys_hbm.at[pl.ds(dest_ref[k, t], 1), :],
                                  buf.at[k, pl.ds(t, 1), :], sem).start()
        return c

    lax.fori_loop(0, tq, issue, 0)

    def drain(t, c):
        for k in range(TOP_K):
            pltpu.make_async_copy(ys_hbm.at[pl.ds(dest_ref[k, t], 1), :],
                                  buf.at[k, pl.ds(t, 1), :], sem).wait()
        return c

    lax.fori_loop(0, tq, drain, 0)

    g = gts_ref[...]
    gt = jnp.concatenate([g, jnp.zeros_like(g)], axis=0).T
    y = gt[:, 0:1] * buf[0]
    for k in range(1, TOP_K):
        y = y + gt[:, k:k + 1] * buf[k]
    xo = x1_ref[...] + gate_ref[...] * y
    ms = jnp.mean(xo * xo, axis=-1, keepdims=True)
    o_ref[...] = xo * lax.rsqrt(ms + EPS) * nf_ref[...]


def combine(dest, ys, gts, x1, gate, norm_final):
    B, S, D = x1.shape
    K = dest.shape[1]
    tq = min(TOKEN_TILE, S)
    lane = lambda ms: pl.BlockSpec((None, K, tq), lambda b, i: (b, 0, i), memory_space=ms)
    tile = pl.BlockSpec((None, tq, D), lambda b, i: (b, i, 0))
    return pl.pallas_call(
        _combine_kernel,
        grid=(B, S // tq),
        in_specs=[lane(pltpu.SMEM), pl.BlockSpec(memory_space=pl.ANY), lane(pltpu.VMEM), tile,
                  pl.BlockSpec((None, 1, D), lambda b, i: (b, 0, 0)), _full((1, D))],
        out_specs=tile,
        out_shape=jax.ShapeDtypeStruct((B, S, D), F32),
        scratch_shapes=[pltpu.VMEM((K, tq, D), F32), pltpu.SemaphoreType.DMA(())],
        compiler_params=_cparams(("arbitrary", "arbitrary")),
        name="combine",
    )(dest, ys, gts, x1, gate, norm_final.reshape(1, D))


def kernel(x, c, ctx, c_ctx, w_ada, b_ada, norm_mix, w_in, conv_a_w, w_out_a, lru_conv_w,
           lru_conv_b, lru_wa, lru_ba, lru_wx, lru_bx, lru_lambda, w_out_b, b_merge, w_o,
           norm_ffn, router_w, router_b, w1, b1, w2, b2, norm_final):
    B, S, D = x.shape
    depth = w_ada.shape[0]
    assert depth == 1, "single-layer block"
    l = 0
    E = router_w.shape[-1]
    n_groups = w_in.shape[-1] // D
    lru_col = 3

    pad = -(B + 1) % SUBLANES
    cond = jnp.concatenate([c, c_ctx[None], jnp.zeros((pad, D), F32)], axis=0)
    mod = adaln(cond, w_ada[l], b_ada[l])
    mx = [mod[:B, i * D:(i + 1) * D].reshape(B, 1, D) for i in range(N_MOD)]
    mc = [jnp.broadcast_to(mod[B:B + 1, i * D:(i + 1) * D].reshape(1, 1, D), (B, 1, D))
          for i in range(2)]

    w_in_b = w_in[l].astype(BF16)
    wg = _gate_weights(lru_wa[l], lru_wx[l])
    lru_args = (lru_conv_w[l], lru_conv_b[l], wg, lru_ba[l], lru_bx[l], lru_lambda[l])

    (cs_lru,) = inproj(ctx, norm_mix[l], mc[0], mc[1], w_in_b[:, lru_col * D:(lru_col + 1) * D], 1)
    h_zero = jnp.zeros((2, B, 1, D), F32)
    _, _, h_ctx = lru(cs_lru, *lru_args, h_zero, BF16)

    cols = inproj(x, norm_mix[l], mx[0], mx[1], w_in_b, n_groups)
    hf, hb, _ = lru(cols[lru_col], *lru_args, h_ctx, BF16)

    x1, fx, idx, gts = merge(cols, hf, hb, x, conv_a_w[l], w_out_a[l].astype(BF16),
                             w_out_b[l].astype(BF16), w_o[l].astype(BF16), b_merge[l], mx[2],
                             norm_ffn[l], mx[3], mx[4], router_w[l].T, router_b[l])

    T = B * S
    n_blk = (T * TOP_K + MOE_BLOCK - 1) // MOE_BLOCK + E
    dest, blk = plan(idx, E, n_blk)
    xs = dispatch(dest, fx, n_blk * MOE_BLOCK)
    w1g, w1l = w1prep(w1[l])
    ys = experts(blk, xs, w1g, w1l,
                 b1[l][:, None, 0::2], b1[l][:, None, 1::2], w2[l].astype(BF16), b2[l][:, None, :])
    return combine(dest, ys, gts, x1, mx[5], norm_final)
```

```python
import functools

import jax
import jax.numpy as jnp
from jax import lax
from jax.experimental import pallas as pl
from jax.experimental.pallas import tpu as pltpu

F32 = jnp.float32
BF16 = jnp.bfloat16

EPS = 1e-6
GRID_W = 64
CONV_W = 3
LRU_CONV_W = 4
LRU_BLOCK_W = 64
LRU_C = 8.0
N_MOD = 6
TOP_K = 4
MOE_BLOCK = 512
SWIGLU_LIMIT = 7.0
SWIGLU_ALPHA = 1.702

MXU_W = 256
SUBLANES = 8
VMEM_LIMIT = 56 * 1024 * 1024
LANES = 128
TOKEN_TILE = 512
LRU_CHUNK = 512
ROW_ALIGN = SUBLANES
RUN_SIZES = tuple(TOKEN_TILE >> s for s in range((TOKEN_TILE // ROW_ALIGN).bit_length()))


def _cparams(sem):
    return pltpu.CompilerParams(dimension_semantics=sem, vmem_limit_bytes=VMEM_LIMIT)


def _full(shape):
    nd = len(shape)
    return pl.BlockSpec(shape, lambda *_: (0,) * nd)


def _adaln_kernel(cond_ref, w_ref, b_ref, o_ref):
    c = cond_ref[...]
    s = c * jax.nn.sigmoid(c)
    o_ref[...] = jnp.dot(s, w_ref[...], preferred_element_type=F32,
                         precision=lax.Precision.HIGHEST) + b_ref[...]


def adaln(cond, w, b):
    R, D = cond.shape
    N = w.shape[1]
    tn = D
    return pl.pallas_call(
        _adaln_kernel,
        grid=(N // tn,),
        in_specs=[pl.BlockSpec((R, D), lambda j: (0, 0)),
                  pl.BlockSpec((D, tn), lambda j: (0, j)),
                  pl.BlockSpec((1, tn), lambda j: (0, j))],
        out_specs=pl.BlockSpec((R, tn), lambda j: (0, j)),
        out_shape=jax.ShapeDtypeStruct((R, N), F32),
        compiler_params=_cparams(("arbitrary",)),
        name="adaln",
    )(cond, w, b.reshape(1, N))


def _inproj_kernel(x_ref, g_ref, shift_ref, scale_ref, w_ref, *out_refs):
    xf = x_ref[...]
    ms = jnp.mean(xf * xf, axis=-1, keepdims=True)
    y = xf * lax.rsqrt(ms + EPS) * g_ref[...]
    h = (y * (1.0 + scale_ref[...]) + shift_ref[...]).astype(BF16)
    for j, o in enumerate(out_refs):
        n = o.shape[-1]
        o[...] = jnp.dot(h, w_ref[:, j * n:(j + 1) * n],
                         preferred_element_type=F32).astype(o.dtype)


def inproj(x, g, shift, scale, w, n_groups):
    B, L, D = x.shape
    tm = min(TOKEN_TILE, L)
    n = w.shape[1] // n_groups
    tile = pl.BlockSpec((None, tm, D), lambda b, i: (b, i, 0))
    vec = pl.BlockSpec((None, 1, D), lambda b, i: (b, 0, 0))
    return pl.pallas_call(
        _inproj_kernel,
        grid=(B, L // tm),
        in_specs=[tile, _full((1, D)), vec, vec,
                  pl.BlockSpec(w.shape, lambda b, i: (0, 0), pipeline_mode=pl.Buffered(1))],
        out_specs=[pl.BlockSpec((None, tm, n), lambda b, i: (b, i, 0))] * n_groups,
        out_shape=[jax.ShapeDtypeStruct((B, L, n), BF16)] * n_groups,
        compiler_params=_cparams(("arbitrary", "arbitrary")),
        name="inproj",
    )(x, g.reshape(1, D), shift, scale, w)


def _softplus(z):
    return jnp.maximum(z, 0.0) + jnp.log1p(jnp.exp(-jnp.abs(z)))


def _lru_kernel(uf_ref, ub_ref, cw_ref, cb_ref, wg_ref, ba_ref, bx_ref, lam_ref, h0_ref,
                hf_ref, hb_ref, hlast_ref,
                uext_f, uext_b, a_s, b_s, h_s, carry_f, carry_b):
    j = pl.program_id(1)
    tc, C = uf_ref.shape
    halo = SUBLANES
    n_groups = C // MXU_W
    nblk = tc // SUBLANES

    @pl.when(j == 0)
    def _():
        uext_f[0:halo, :] = jnp.zeros((halo, C), F32)
        uext_b[tc:tc + halo, :] = jnp.zeros((halo, C), F32)
        carry_f[...] = h0_ref[0]
        carry_b[...] = h0_ref[1]

    row = lax.broadcasted_iota(jnp.int32, (SUBLANES, C), 0)

    def one_dir(d, u_ref, uext, o_ref, carry, reverse):
        if reverse:
            uext[0:tc, :] = u_ref[...].astype(F32)
            offs = [LRU_CONV_W - 1 - q for q in range(LRU_CONV_W)]
        else:
            uext[halo:halo + tc, :] = u_ref[...].astype(F32)
            offs = [halo - (LRU_CONV_W - 1) + q for q in range(LRU_CONV_W)]
        v = cb_ref[pl.ds(d, 1), :]
        for q in range(LRU_CONV_W):
            v = v + cw_ref[d, pl.ds(q, 1), :] * uext[pl.ds(offs[q], tc), :]
        if reverse:
            uext[tc:tc + halo, :] = uext[0:halo, :]
        else:
            uext[0:halo, :] = uext[tc:tc + halo, :]

        c8 = -LRU_C * _softplus(-lam_ref[pl.ds(d, 1), :])
        vb = v.astype(BF16)
        for g in range(n_groups):
            cols = slice(g * MXU_W, (g + 1) * MXU_W)
            rg = jnp.dot(vb[:, cols], wg_ref[d, g], preferred_element_type=F32)
            r = jax.nn.sigmoid(rg[:, :MXU_W] + ba_ref[pl.ds(d, 1), cols])
            i = jax.nn.sigmoid(rg[:, MXU_W:] + bx_ref[pl.ds(d, 1), cols])
            log_a = c8[:, cols] * r
            a = jnp.exp(log_a)
            a_s[:, cols] = a
            b_s[:, cols] = jnp.sqrt(-jnp.tanh(log_a) * (a * a + 1.0)) * (i * v[:, cols])

        def body(k, h):
            kk = (nblk - 1 - k) if reverse else k
            sl = pl.ds(pl.multiple_of(kk * SUBLANES, SUBLANES), SUBLANES)
            a = a_s[sl, :]
            b = b_s[sl, :]
            for s in (1, 2, 4):
                if reverse:
                    m = row < SUBLANES - s
                    sh = SUBLANES - s
                else:
                    m = row >= s
                    sh = s
                a_sh = jnp.where(m, pltpu.roll(a, sh, 0), 1.0)
                b_sh = jnp.where(m, pltpu.roll(b, sh, 0), 0.0)
                b = a * b_sh + b
                a = a * a_sh
            hblk = a * h + b
            h_s[sl, :] = hblk
            return hblk[0:1, :] if reverse else hblk[SUBLANES - 1:SUBLANES, :]

        h = lax.fori_loop(0, nblk, body, carry[...])
        carry[...] = h
        o_ref[...] = h_s[...].astype(o_ref.dtype)

    one_dir(0, uf_ref, uext_f, hf_ref, carry_f, False)
    one_dir(1, ub_ref, uext_b, hb_ref, carry_b, True)
    hlast_ref[0] = carry_f[...]
    hlast_ref[1] = carry_b[...]


def lru(u, conv_w, conv_b, wg, ba, bx, lam, h0, out_dtype):
    B, L, C = u.shape
    tc = min(LRU_CHUNK, L)
    n = L // tc
    blk_f = pl.BlockSpec((None, tc, C), lambda b, j: (b, j, 0))
    blk_b = pl.BlockSpec((None, tc, C), lambda b, j: (b, n - 1 - j, 0))
    state = pl.BlockSpec((2, None, 1, C), lambda b, j: (0, b, 0, 0))
    return pl.pallas_call(
        _lru_kernel,
        grid=(B, n),
        in_specs=[blk_f, blk_b, _full(conv_w.shape), _full(conv_b.shape), _full(wg.shape),
                  _full(ba.shape), _full(bx.shape), _full(lam.shape), state],
        out_specs=[blk_f, blk_b, state],
        out_shape=[jax.ShapeDtypeStruct((B, L, C), out_dtype),
                   jax.ShapeDtypeStruct((B, L, C), out_dtype),
                   jax.ShapeDtypeStruct((2, B, 1, C), F32)],
        scratch_shapes=[pltpu.VMEM((tc + SUBLANES, C), F32), pltpu.VMEM((tc + SUBLANES, C), F32),
                        pltpu.VMEM((tc, C), F32), pltpu.VMEM((tc, C), F32), pltpu.VMEM((tc, C), F32),
                        pltpu.VMEM((1, C), F32), pltpu.VMEM((1, C), F32)],
        compiler_params=_cparams(("arbitrary", "arbitrary")),
        name="lru",
    )(u, u, conv_w, conv_b, wg, ba, bx, lam, h0)


def _gate_weights(wa, wx):
    per = MXU_W // LRU_BLOCK_W
    nd, nb, k, _ = wa.shape
    eye = jnp.eye(per, dtype=wa.dtype)

    def bd(w):
        w = w.reshape(nd, nb // per, per, k, k)
        return jnp.einsum('dgpkj,pq->dgpkqj', w, eye).reshape(nd, nb // per, MXU_W, MXU_W)

    return jnp.concatenate([bd(wa), bd(wx)], axis=-1).astype(BF16)


def _merge_kernel(gb_ref, gc_ref, xc_ref, gcp_ref, xcp_ref, gcn_ref, xcn_ref,
                  lg_ref, hf_ref, hb_ref, ma_ref, mb_ref, x_ref,
                  cw_ref, woa_ref, wob_ref, wo_ref, bm_ref, gate_ref, nf_ref, sh_ref, sc_ref,
                  rwt_ref, rb_ref,
                  x1_ref, fx_ref, idx_ref, gts_ref):
    i = pl.program_id(1)
    n = pl.num_programs(1)
    tm, C = gc_ref.shape
    half = C // 2
    E = rwt_ref.shape[0]

    u = gc_ref[...].astype(F32) * xc_ref[...].astype(F32)
    uh = u[:, :half]
    col = lax.broadcasted_iota(jnp.int32, (tm, half), 0) % GRID_W
    left = jnp.where(col >= 1, pltpu.roll(uh, 1, 0), 0.0)
    right = jnp.where(col <= GRID_W - 2, pltpu.roll(uh, tm - 1, 0), 0.0)
    yh = (cw_ref[0:1, :half] * left + cw_ref[1:2, :half] * uh + cw_ref[2:3, :half] * right)
    uv = u[:, half:]
    prev = gcp_ref[...].astype(F32) * xcp_ref[...].astype(F32)
    nxt = gcn_ref[...].astype(F32) * xcn_ref[...].astype(F32)
    prev = jnp.where(i > 0, prev, 0.0)
    nxt = jnp.where(i < n - 1, nxt, 0.0)
    up = jnp.concatenate([prev, uv[:tm - GRID_W]], axis=0)
    down = jnp.concatenate([uv[GRID_W:], nxt], axis=0)
    yv = (cw_ref[0:1, half:] * up + cw_ref[1:2, half:] * uv + cw_ref[2:3, half:] * down)

    gb = gb_ref[...].astype(F32)
    za_h = (gb[:, :half] * yh).astype(BF16)
    za_v = (gb[:, half:] * yv).astype(BF16)
    y_a = (jnp.dot(za_h, woa_ref[:half, :], preferred_element_type=F32)
           + jnp.dot(za_v, woa_ref[half:, :], preferred_element_type=F32))

    h_lru = hf_ref[...].astype(F32) + hb_ref[...].astype(F32)
    zb = (jax.nn.gelu(lg_ref[...].astype(F32), approximate=True) * h_lru).astype(BF16)
    y_b = jnp.dot(zb, wob_ref[...], preferred_element_type=F32)

    merged = (jax.nn.sigmoid(ma_ref[...].astype(F32) + bm_ref[0:1, :]) * y_a
              + jax.nn.sigmoid(mb_ref[...].astype(F32) + bm_ref[1:2, :]) * y_b)
    mix = jnp.dot(merged.astype(BF16), wo_ref[...], preferred_element_type=F32)
    x1 = x_ref[...] + gate_ref[...] * mix
    x1_ref[...] = x1

    ms = jnp.mean(x1 * x1, axis=-1, keepdims=True)
    fx = (x1 * lax.rsqrt(ms + EPS) * nf_ref[...]) * (1.0 + sc_ref[...]) + sh_ref[...]
    fx_ref[...] = fx

    logits = lax.dot_general(rwt_ref[...], fx, (((1,), (1,)), ((), ())),
                             preferred_element_type=F32,
                             precision=lax.Precision.HIGHEST) + rb_ref[...]
    erow = lax.broadcasted_iota(jnp.int32, (E, tm), 0).astype(F32)
    vals = []
    for k in range(TOP_K):
        m = jnp.max(logits, axis=0, keepdims=True)
        am = jnp.min(jnp.where(logits == m, erow, float(E)), axis=0, keepdims=True)
        vals.append(m)
        idx_ref[k:k + 1, :] = am.astype(jnp.int32)
        logits = jnp.where(erow == am, -jnp.inf, logits)
    ex = [jnp.exp(v - vals[0]) for v in vals]
    den = ex[0] + ex[1] + ex[2] + ex[3]
    for k in range(TOP_K):
        gts_ref[k:k + 1, :] = ex[k] / den


def merge(cols, hf, hb, x, conv_w, w_out_a, w_out_b, w_o, b_merge, gate, norm_ffn, shift, scale,
          router_wt, router_b):
    gb, gc, xc, _, lg, ma, mb = cols
    B, S, D = x.shape
    C = gc.shape[-1]
    half = C // 2
    E = router_wt.shape[0]
    tm = min(TOKEN_TILE, S)
    n = S // tm
    r = tm // GRID_W
    n_rows = S // GRID_W
    tile = lambda w: pl.BlockSpec((None, tm, w), lambda b, i: (b, i, 0))
    prev = pl.BlockSpec((None, GRID_W, half), lambda b, i: (b, jnp.maximum(i * r - 1, 0), 1))
    nxt = pl.BlockSpec((None, GRID_W, half),
                       lambda b, i: (b, jnp.minimum((i + 1) * r, n_rows - 1), 1))
    vec = pl.BlockSpec((None, 1, D), lambda b, i: (b, 0, 0))
    lane_out = pl.BlockSpec((None, TOP_K, tm), lambda b, i: (b, 0, i))
    return pl.pallas_call(
        _merge_kernel,
        grid=(B, n),
        in_specs=[tile(C), tile(C), tile(C), prev, prev, nxt, nxt,
                  tile(C), tile(C), tile(C), tile(D), tile(D), tile(D),
                  _full(conv_w.shape), _full(w_out_a.shape), _full(w_out_b.shape),
                  _full(w_o.shape), _full(b_merge.shape), vec, _full((1, D)), vec, vec,
                  _full(router_wt.shape), _full((E, 1))],
        out_specs=[tile(D), tile(D), lane_out, lane_out],
        out_shape=[jax.ShapeDtypeStruct((B, S, D), F32), jax.ShapeDtypeStruct((B, S, D), F32),
                   jax.ShapeDtypeStruct((B, TOP_K, S), jnp.int32),
                   jax.ShapeDtypeStruct((B, TOP_K, S), F32)],
        compiler_params=_cparams(("arbitrary", "arbitrary")),
        name="merge",
    )(gb, gc, xc, gc, xc, gc, xc, lg, hf, hb, ma, mb, x,
      conv_w, w_out_a, w_out_b, w_o, b_merge, gate, norm_ffn.reshape(1, D), shift, scale,
      router_wt, router_b.reshape(E, 1))


def _selection(idx_ref, n_experts):
    tp = idx_ref.shape[1]
    idx = idx_ref[...]
    erow = lax.broadcasted_iota(jnp.int32, (n_experts, tp), 0)
    sel = [erow == idx[k:k + 1, :] for k in range(TOP_K)]
    onehot = sum(s.astype(F32) for s in sel)
    tile_cnt = jnp.sum(onehot, axis=1, keepdims=True).astype(jnp.int32)
    return sel, onehot, tile_cnt


def _round_up(v, m):
    s = m.bit_length() - 1
    return ((v + (m - 1)) >> s) << s


def _excl_cumsum(col):
    E = col.shape[0]
    r_i = lax.broadcasted_iota(jnp.int32, (E, E), 0)
    c_i = lax.broadcasted_iota(jnp.int32, (E, E), 1)
    row = jnp.sum(jnp.where(r_i == c_i, col, 0).astype(F32), axis=0, keepdims=True)
    return jnp.sum(jnp.where(c_i < r_i, row, 0.0), axis=1, keepdims=True).astype(jnp.int32)


def _tile_index():
    return pl.program_id(0) * pl.num_programs(1) + pl.program_id(1)


def _count_kernel(idx_ref, cp_ref):
    i = _tile_index()
    _, _, tile_cnt = _selection(idx_ref, cp_ref.shape[0])

    @pl.when(i == 0)
    def _():
        cp_ref[...] = jnp.zeros_like(cp_ref)

    lane = lax.broadcasted_iota(jnp.int32, cp_ref.shape, 1)
    cp_ref[...] += jnp.where(lane == i, _round_up(tile_cnt, ROW_ALIGN), 0)


def _plan_kernel(idx_ref, cp_ref, lpos_ref, tab_ref, blk_ref, base_s, tri_s):
    i = _tile_index()
    tp = idx_ref.shape[1]
    E = cp_ref.shape[0]
    n_blk = blk_ref.shape[1]
    sel, onehot, tile_cnt = _selection(idx_ref, E)
    run = _round_up(tile_cnt, ROW_ALIGN)

    @pl.when(i == 0)
    def _():
        used = jnp.sum(cp_ref[...].astype(F32), axis=1, keepdims=True).astype(jnp.int32)
        region = _round_up(used, MOE_BLOCK)
        pstart = _excl_cumsum(region)
        base_s[...] = pstart
        bs = lax.broadcasted_iota(jnp.int32, (E, n_blk), 1) * MOE_BLOCK
        inb = (bs >= pstart) & (bs < pstart + region)
        e_i = lax.broadcasted_iota(jnp.int32, (E, n_blk), 0)
        valid = jnp.clip(used - (bs - pstart), 0, MOE_BLOCK)
        blk_ref[0:1, :] = jnp.sum(jnp.where(inb, e_i, 0).astype(F32), axis=0,
                                  keepdims=True).astype(jnp.int32)
        blk_ref[1:2, :] = jnp.sum(jnp.where(inb, valid, 0).astype(F32), axis=0,
                                  keepdims=True).astype(jnp.int32)
        t_r = lax.broadcasted_iota(jnp.int32, (tp, tp), 0)
        t_c = lax.broadcasted_iota(jnp.int32, (tp, tp), 1)
        tri_s[...] = (t_r < t_c).astype(BF16)
        tab_ref[...] = jnp.zeros_like(tab_ref)

    loff = _excl_cumsum(run)
    pref = jnp.dot(onehot.astype(BF16), tri_s[...], preferred_element_type=F32)
    pos = pref + loff.astype(F32)
    for k in range(TOP_K):
        lpos_ref[k:k + 1, :] = jnp.sum(jnp.where(sel[k], pos, 0.0), axis=0,
                                       keepdims=True).astype(jnp.int32)

    hit = lax.broadcasted_iota(jnp.int32, tab_ref.shape[1:], 1) == i
    tab_ref[0] += jnp.where(hit, run, 0)
    tab_ref[1] += jnp.where(hit, loff, 0)
    tab_ref[2] += jnp.where(hit, base_s[...], 0)
    base_s[...] += run


def plan(idx, n_experts, n_blk):
    B, K, S = idx.shape
    tp = min(TOKEN_TILE, S)
    n_tiles = B * (S // tp)
    ntp = -(-n_tiles // LANES) * LANES
    lane = pl.BlockSpec((None, K, tp), lambda b, i: (b, 0, i))
    cp = pl.pallas_call(
        _count_kernel,
        grid=(B, S // tp),
        in_specs=[lane],
        out_specs=_full((n_experts, ntp)),
        out_shape=jax.ShapeDtypeStruct((n_experts, ntp), jnp.int32),
        compiler_params=_cparams(("arbitrary", "arbitrary")),
        name="count",
    )(idx)
    return pl.pallas_call(
        _plan_kernel,
        grid=(B, S // tp),
        in_specs=[lane, _full((n_experts, ntp))],
        out_specs=[lane, _full((3, n_experts, ntp)), _full((2, n_blk))],
        out_shape=[jax.ShapeDtypeStruct((B, K, S), jnp.int32),
                   jax.ShapeDtypeStruct((3, n_experts, ntp), jnp.int32),
                   jax.ShapeDtypeStruct((2, n_blk), jnp.int32)],
        scratch_shapes=[pltpu.VMEM((n_experts, 1), jnp.int32), pltpu.VMEM((tp, tp), BF16)],
        compiler_params=_cparams(("arbitrary", "arbitrary")),
        name="plan",
    )(idx, cp)


def _sorted_rows(tq, n_experts):
    return _round_up(TOP_K * tq + n_experts * (ROW_ALIGN - 1), ROW_ALIGN)


def _for_each_run_piece(tab_ref, i, fn):
    def per_expert(e, carry):
        run = tab_ref[0, e, i]
        done = jnp.int32(0)
        for size in RUN_SIZES:
            hit = (run & size) != 0

            @pl.when(hit)
            def _():
                fn(pl.multiple_of(tab_ref[1, e, i] + done, ROW_ALIGN),
                   pl.multiple_of(tab_ref[2, e, i] + done, ROW_ALIGN), size)

            done = done + jnp.where(hit, size, 0)
        return carry

    lax.fori_loop(0, tab_ref.shape[1], per_expert, 0)


def _dispatch_kernel(tab_ref, lpos_ref, fx_ref, xs_hbm, sbuf, sem):
    i = pl.program_id(0)
    rb, tq = sbuf.shape[0], fx_ref.shape[0]
    lpos = lpos_ref[...]
    r = lax.broadcasted_iota(jnp.int32, (rb, tq), 0)
    hit = r == lpos[0:1, :]
    for k in range(1, TOP_K):
        hit = hit | (r == lpos[k:k + 1, :])
    perm = jnp.where(hit, 1.0, 0.0).astype(BF16)
    sbuf[...] = jnp.dot(perm, fx_ref[...].astype(BF16),
                        preferred_element_type=F32).astype(sbuf.dtype)

    def copy(lo, off, size):
        return pltpu.make_async_copy(sbuf.at[pl.ds(lo, size), :], xs_hbm.at[pl.ds(off, size), :], sem)

    _for_each_run_piece(tab_ref, i, lambda lo, off, size: copy(lo, off, size).start())
    _for_each_run_piece(tab_ref, i, lambda lo, off, size: copy(lo, off, size).wait())


def dispatch(tab, lpos, fx, n_rows):
    B, K, S = lpos.shape
    D = fx.shape[-1]
    E = tab.shape[1]
    tq = min(TOKEN_TILE, S)
    n = S // tq
    return pl.pallas_call(
        _dispatch_kernel,
        grid_spec=pltpu.PrefetchScalarGridSpec(
            num_scalar_prefetch=1,
            grid=(B * n,),
            in_specs=[pl.BlockSpec((None, K, tq), lambda i, tab: (i // n, 0, i % n)),
                      pl.BlockSpec((tq, D), lambda i, tab: (i, 0))],
            out_specs=pl.BlockSpec(memory_space=pl.ANY),
            scratch_shapes=[pltpu.VMEM((_sorted_rows(tq, E), D), F32),
                            pltpu.SemaphoreType.DMA(())]),
        out_shape=jax.ShapeDtypeStruct((n_rows, D), F32),
        compiler_params=_cparams(("arbitrary",)),
        name="dispatch",
    )(tab, lpos, fx.reshape(B * S, D))


def _w1prep_kernel(w_ref, g_ref, l_ref):
    half = MXU_W // 2
    r = lax.broadcasted_iota(jnp.int32, (MXU_W, MXU_W), 0)
    c = lax.broadcasted_iota(jnp.int32, (MXU_W, MXU_W), 1)
    perm = (r == jnp.where(c < half, 2 * c, 2 * (c - half) + 1)).astype(BF16)
    for q in range(w_ref.shape[1] // MXU_W):
        blk = w_ref[:, q * MXU_W:(q + 1) * MXU_W].astype(BF16)
        out = jnp.dot(blk, perm, preferred_element_type=F32)
        g_ref[:, q * half:(q + 1) * half] = out[:, :half].astype(BF16)
        l_ref[:, q * half:(q + 1) * half] = out[:, half:].astype(BF16)


def w1prep(w1):
    E, D, F2 = w1.shape
    td = min(TOKEN_TILE, D)
    out = pl.BlockSpec((None, td, F2 // 2), lambda e, i: (e, i, 0))
    return pl.pallas_call(
        _w1prep_kernel,
        grid=(E, D // td),
        in_specs=[pl.BlockSpec((None, td, F2), lambda e, i: (e, i, 0))],
        out_specs=[out, out],
        out_shape=[jax.ShapeDtypeStruct((E, D, F2 // 2), BF16)] * 2,
        compiler_params=_cparams(("arbitrary", "arbitrary")),
        name="w1prep",
    )(w1)


def _expert_kernel(blk_ref, x_ref, w1g_ref, w1l_ref, b1g_ref, b1l_ref, w2_ref, b2_ref, y_ref):
    j = pl.program_id(0)
    nvalid = blk_ref[1, j]

    @pl.when(nvalid > 0)
    def _():
        rows = lax.broadcasted_iota(jnp.int32, x_ref.shape, 0)
        x = jnp.where(rows < nvalid, x_ref[...], 0.0).astype(BF16)
        hg = jnp.dot(x, w1g_ref[...], preferred_element_type=F32) + b1g_ref[...]
        hl = jnp.dot(x, w1l_ref[...], preferred_element_type=F32) + b1l_ref[...]
        glu = jnp.minimum(hg, SWIGLU_LIMIT)
        lin = jnp.clip(hl, -SWIGLU_LIMIT, SWIGLU_LIMIT)
        act = glu * jax.nn.sigmoid(SWIGLU_ALPHA * glu) * (lin + 1.0)
        y_ref[...] = jnp.dot(act.astype(BF16), w2_ref[...], preferred_element_type=F32) + b2_ref[...]

    @pl.when(nvalid == 0)
    def _():
        y_ref[...] = jnp.zeros_like(y_ref)


def experts(blk, xs, w1g, w1l, b1g, b1l, w2, b2):
    n_rows, D = xs.shape
    E, _, F = w1g.shape
    n_blk = n_rows // MOE_BLOCK
    rows = pl.BlockSpec((MOE_BLOCK, D), lambda j, blk: (j, 0))
    wsel = lambda a, b: pl.BlockSpec((None, a, b), lambda j, blk: (blk[0, j], 0, 0))
    return pl.pallas_call(
        _expert_kernel,
        grid_spec=pltpu.PrefetchScalarGridSpec(
            num_scalar_prefetch=1,
            grid=(n_blk,),
            in_specs=[rows, wsel(D, F), wsel(D, F), wsel(1, F), wsel(1, F), wsel(F, D), wsel(1, D)],
            out_specs=rows),
        out_shape=jax.ShapeDtypeStruct((n_rows, D), F32),
        compiler_params=_cparams(("arbitrary",)),
        name="experts",
    )(blk, xs, w1g, w1l, b1g, b1l, w2, b2)


def _combine_kernel(tab_ref, lpos_ref, gts_ref, ys_hbm, x1_ref, gate_ref, nf_ref, o_ref, ybuf, sem):
    i = pl.program_id(0)
    tq = x1_ref.shape[0]
    rb = ybuf.shape[0]

    @pl.when(i == 0)
    def _():
        ybuf[...] = jnp.zeros_like(ybuf)

    def copy(lo, off, size):
        return pltpu.make_async_copy(ys_hbm.at[pl.ds(off, size), :], ybuf.at[pl.ds(lo, size), :], sem)

    _for_each_run_piece(tab_ref, i, lambda lo, off, size: copy(lo, off, size).start())
    _for_each_run_piece(tab_ref, i, lambda lo, off, size: copy(lo, off, size).wait())

    lg = jnp.concatenate([lpos_ref[...].astype(F32), gts_ref[...]], axis=0).T
    c = lax.broadcasted_iota(jnp.int32, (tq, rb), 1).astype(F32)
    w = jnp.zeros((tq, rb), F32)
    for k in range(TOP_K):
        w = jnp.where(c == lg[:, k:k + 1], lg[:, TOP_K + k:TOP_K + k + 1], w)
    y = jnp.dot(w.astype(BF16), ybuf[...].astype(BF16), preferred_element_type=F32)
    xo = x1_ref[...] + gate_ref[...] * y
    ms = jnp.mean(xo * xo, axis=-1, keepdims=True)
    o_ref[...] = xo * lax.rsqrt(ms + EPS) * nf_ref[...]


def combine(tab, lpos, gts, ys, x1, gate, norm_final):
    B, S, D = x1.shape
    K = lpos.shape[1]
    E = tab.shape[1]
    tq = min(TOKEN_TILE, S)
    n = S // tq
    lane = pl.BlockSpec((None, K, tq), lambda i, tab: (i // n, 0, i % n))
    tile = pl.BlockSpec((tq, D), lambda i, tab: (i, 0))
    out = pl.pallas_call(
        _combine_kernel,
        grid_spec=pltpu.PrefetchScalarGridSpec(
            num_scalar_prefetch=1,
            grid=(B * n,),
            in_specs=[lane, lane, pl.BlockSpec(memory_space=pl.ANY), tile,
                      pl.BlockSpec((None, 1, D), lambda i, tab: (i // n, 0, 0)),
                      pl.BlockSpec((1, D), lambda i, tab: (0, 0))],
            out_specs=tile,
            scratch_shapes=[pltpu.VMEM((_sorted_rows(tq, E), D), F32),
                            pltpu.SemaphoreType.DMA(())]),
        out_shape=jax.ShapeDtypeStruct((B * S, D), F32),
        compiler_params=_cparams(("arbitrary",)),
        name="combine",
    )(tab, lpos, gts, ys, x1.reshape(B * S, D), gate, norm_final.reshape(1, D))
    return out.reshape(B, S, D)


def kernel(x, c, ctx, c_ctx, w_ada, b_ada, norm_mix, w_in, conv_a_w, w_out_a, lru_conv_w,
           lru_conv_b, lru_wa, lru_ba, lru_wx, lru_bx, lru_lambda, w_out_b, b_merge, w_o,
           norm_ffn, router_w, router_b, w1, b1, w2, b2, norm_final):
    B, S, D = x.shape
    depth = w_ada.shape[0]
    assert depth == 1, "single-layer block"
    l = 0
    E = router_w.shape[-1]
    n_groups = w_in.shape[-1] // D
    lru_col = 3

    pad = -(B + 1) % SUBLANES
    cond = jnp.concatenate([c, c_ctx[None], jnp.zeros((pad, D), F32)], axis=0)
    mod = adaln(cond, w_ada[l], b_ada[l])
    mx = [mod[:B, i * D:(i + 1) * D].reshape(B, 1, D) for i in range(N_MOD)]
    mc = [jnp.broadcast_to(mod[B:B + 1, i * D:(i + 1) * D].reshape(1, 1, D), (B, 1, D))
          for i in range(2)]

    w_in_b = w_in[l].astype(BF16)
    wg = _gate_weights(lru_wa[l], lru_wx[l])
    lru_args = (lru_conv_w[l], lru_conv_b[l], wg, lru_ba[l], lru_bx[l], lru_lambda[l])

    (cs_lru,) = inproj(ctx, norm_mix[l], mc[0], mc[1], w_in_b[:, lru_col * D:(lru_col + 1) * D], 1)
    h_zero = jnp.zeros((2, B, 1, D), F32)
    _, _, h_ctx = lru(cs_lru, *lru_args, h_zero, BF16)

    cols = inproj(x, norm_mix[l], mx[0], mx[1], w_in_b, n_groups)
    hf, hb, _ = lru(cols[lru_col], *lru_args, h_ctx, BF16)

    x1, fx, idx, gts = merge(cols, hf, hb, x, conv_a_w[l], w_out_a[l].astype(BF16),
                             w_out_b[l].astype(BF16), w_o[l].astype(BF16), b_merge[l], mx[2],
                             norm_ffn[l], mx[3], mx[4], router_w[l].T, router_b[l])

    T = B * S
    n_tiles = T // min(TOKEN_TILE, S)
    n_blk = -(-(T * TOP_K + n_tiles * E * (ROW_ALIGN - 1)) // MOE_BLOCK) + E
    lpos, tab, blk = plan(idx, E, n_blk)
    xs = dispatch(tab, lpos, fx, n_blk * MOE_BLOCK)
    w1g, w1l = w1prep(w1[l])
    ys = experts(blk, xs, w1g, w1l,
                 b1[l][:, None, 0::2], b1[l][:, None, 1::2], w2[l].astype(BF16), b2[l][:, None, :])
    return combine(tab, lpos, gts, ys, x1, mx[5], norm_final)
```

```python
import functools

import jax
import jax.numpy as jnp
from jax import lax
from jax.experimental import pallas as pl
from jax.experimental.pallas import tpu as pltpu

F32 = jnp.float32
BF16 = jnp.bfloat16

EPS = 1e-6
GRID_W = 64
CONV_W = 3
LRU_CONV_W = 4
LRU_BLOCK_W = 64
LRU_C = 8.0
N_MOD = 6
TOP_K = 4
MOE_BLOCK = 512
SWIGLU_LIMIT = 7.0
SWIGLU_ALPHA = 1.702

MXU_W = 256
SUBLANES = 8
VMEM_LIMIT = 56 * 1024 * 1024
LANES = 128
TOKEN_TILE = 512
LRU_CHUNK = 512
ROW_ALIGN = SUBLANES
assert MOE_BLOCK == TOKEN_TILE
RUN_SIZES = tuple(TOKEN_TILE >> s for s in range((TOKEN_TILE // ROW_ALIGN).bit_length()))


def _cparams(sem):
    return pltpu.CompilerParams(dimension_semantics=sem, vmem_limit_bytes=VMEM_LIMIT)


def _full(shape):
    nd = len(shape)
    return pl.BlockSpec(shape, lambda *_: (0,) * nd)


def _adaln_kernel(cond_ref, w_ref, b_ref, o_ref):
    c = cond_ref[...]
    s = c * jax.nn.sigmoid(c)
    o_ref[...] = jnp.dot(s, w_ref[...], preferred_element_type=F32,
                         precision=lax.Precision.HIGHEST) + b_ref[...]


def adaln(cond, w, b):
    R, D = cond.shape
    N = w.shape[1]
    tn = D
    return pl.pallas_call(
        _adaln_kernel,
        grid=(N // tn,),
        in_specs=[pl.BlockSpec((R, D), lambda j: (0, 0)),
                  pl.BlockSpec((D, tn), lambda j: (0, j)),
                  pl.BlockSpec((1, tn), lambda j: (0, j))],
        out_specs=pl.BlockSpec((R, tn), lambda j: (0, j)),
        out_shape=jax.ShapeDtypeStruct((R, N), F32),
        compiler_params=_cparams(("arbitrary",)),
        name="adaln",
    )(cond, w, b.reshape(1, N))


def _inproj_kernel(x_ref, g_ref, shift_ref, scale_ref, w_ref, *out_refs):
    xf = x_ref[...]
    ms = jnp.mean(xf * xf, axis=-1, keepdims=True)
    y = xf * lax.rsqrt(ms + EPS) * g_ref[...]
    h = (y * (1.0 + scale_ref[...]) + shift_ref[...]).astype(BF16)
    for j, o in enumerate(out_refs):
        n = o.shape[-1]
        o[...] = jnp.dot(h, w_ref[:, j * n:(j + 1) * n],
                         preferred_element_type=F32).astype(o.dtype)


def inproj(x, g, shift, scale, w, n_groups):
    B, L, D = x.shape
    tm = min(TOKEN_TILE, L)
    n = w.shape[1] // n_groups
    tile = pl.BlockSpec((None, tm, D), lambda b, i: (b, i, 0))
    vec = pl.BlockSpec((None, 1, D), lambda b, i: (b, 0, 0))
    return pl.pallas_call(
        _inproj_kernel,
        grid=(B, L // tm),
        in_specs=[tile, _full((1, D)), vec, vec,
                  pl.BlockSpec(w.shape, lambda b, i: (0, 0), pipeline_mode=pl.Buffered(1))],
        out_specs=[pl.BlockSpec((None, tm, n), lambda b, i: (b, i, 0))] * n_groups,
        out_shape=[jax.ShapeDtypeStruct((B, L, n), BF16)] * n_groups,
        compiler_params=_cparams(("arbitrary", "arbitrary")),
        name="inproj",
    )(x, g.reshape(1, D), shift, scale, w)


def _softplus(z):
    return jnp.maximum(z, 0.0) + jnp.log1p(jnp.exp(-jnp.abs(z)))


def _lru_kernel(uf_ref, ub_ref, cw_ref, cb_ref, wg_ref, ba_ref, bx_ref, lam_ref, h0_ref,
                hf_ref, hb_ref, hlast_ref,
                uext_f, uext_b, a_s, b_s, h_s, carry_f, carry_b):
    j = pl.program_id(1)
    tc, C = uf_ref.shape
    halo = SUBLANES
    n_groups = C // MXU_W
    nblk = tc // SUBLANES

    @pl.when(j == 0)
    def _():
        uext_f[0:halo, :] = jnp.zeros((halo, C), F32)
        uext_b[tc:tc + halo, :] = jnp.zeros((halo, C), F32)
        carry_f[...] = h0_ref[0]
        carry_b[...] = h0_ref[1]

    row = lax.broadcasted_iota(jnp.int32, (SUBLANES, C), 0)

    def one_dir(d, u_ref, uext, o_ref, carry, reverse):
        if reverse:
            uext[0:tc, :] = u_ref[...].astype(F32)
            offs = [LRU_CONV_W - 1 - q for q in range(LRU_CONV_W)]
        else:
            uext[halo:halo + tc, :] = u_ref[...].astype(F32)
            offs = [halo - (LRU_CONV_W - 1) + q for q in range(LRU_CONV_W)]
        v = cb_ref[pl.ds(d, 1), :]
        for q in range(LRU_CONV_W):
            v = v + cw_ref[d, pl.ds(q, 1), :] * uext[pl.ds(offs[q], tc), :]
        if reverse:
            uext[tc:tc + halo, :] = uext[0:halo, :]
        else:
            uext[0:halo, :] = uext[tc:tc + halo, :]

        c8 = -LRU_C * _softplus(-lam_ref[pl.ds(d, 1), :])
        vb = v.astype(BF16)
        for g in range(n_groups):
            cols = slice(g * MXU_W, (g + 1) * MXU_W)
            rg = jnp.dot(vb[:, cols], wg_ref[d, g], preferred_element_type=F32)
            r = jax.nn.sigmoid(rg[:, :MXU_W] + ba_ref[pl.ds(d, 1), cols])
            i = jax.nn.sigmoid(rg[:, MXU_W:] + bx_ref[pl.ds(d, 1), cols])
            log_a = c8[:, cols] * r
            a = jnp.exp(log_a)
            a_s[:, cols] = a
            b_s[:, cols] = jnp.sqrt(-jnp.tanh(log_a) * (a * a + 1.0)) * (i * v[:, cols])

        def body(k, h):
            kk = (nblk - 1 - k) if reverse else k
            sl = pl.ds(pl.multiple_of(kk * SUBLANES, SUBLANES), SUBLANES)
            a = a_s[sl, :]
            b = b_s[sl, :]
            for s in (1, 2, 4):
                if reverse:
                    m = row < SUBLANES - s
                    sh = SUBLANES - s
                else:
                    m = row >= s
                    sh = s
                a_sh = jnp.where(m, pltpu.roll(a, sh, 0), 1.0)
                b_sh = jnp.where(m, pltpu.roll(b, sh, 0), 0.0)
                b = a * b_sh + b
                a = a * a_sh
            hblk = a * h + b
            h_s[sl, :] = hblk
            return hblk[0:1, :] if reverse else hblk[SUBLANES - 1:SUBLANES, :]

        h = lax.fori_loop(0, nblk, body, carry[...])
        carry[...] = h
        o_ref[...] = h_s[...].astype(o_ref.dtype)

    one_dir(0, uf_ref, uext_f, hf_ref, carry_f, False)
    one_dir(1, ub_ref, uext_b, hb_ref, carry_b, True)
    hlast_ref[0] = carry_f[...]
    hlast_ref[1] = carry_b[...]


def lru(u, conv_w, conv_b, wg, ba, bx, lam, h0, out_dtype):
    B, L, C = u.shape
    tc = min(LRU_CHUNK, L)
    n = L // tc
    blk_f = pl.BlockSpec((None, tc, C), lambda b, j: (b, j, 0))
    blk_b = pl.BlockSpec((None, tc, C), lambda b, j: (b, n - 1 - j, 0))
    state = pl.BlockSpec((2, None, 1, C), lambda b, j: (0, b, 0, 0))
    return pl.pallas_call(
        _lru_kernel,
        grid=(B, n),
        in_specs=[blk_f, blk_b, _full(conv_w.shape), _full(conv_b.shape), _full(wg.shape),
                  _full(ba.shape), _full(bx.shape), _full(lam.shape), state],
        out_specs=[blk_f, blk_b, state],
        out_shape=[jax.ShapeDtypeStruct((B, L, C), out_dtype),
                   jax.ShapeDtypeStruct((B, L, C), out_dtype),
                   jax.ShapeDtypeStruct((2, B, 1, C), F32)],
        scratch_shapes=[pltpu.VMEM((tc + SUBLANES, C), F32), pltpu.VMEM((tc + SUBLANES, C), F32),
                        pltpu.VMEM((tc, C), F32), pltpu.VMEM((tc, C), F32), pltpu.VMEM((tc, C), F32),
                        pltpu.VMEM((1, C), F32), pltpu.VMEM((1, C), F32)],
        compiler_params=_cparams(("arbitrary", "arbitrary")),
        name="lru",
    )(u, u, conv_w, conv_b, wg, ba, bx, lam, h0)


def _gate_weights(wa, wx):
    per = MXU_W // LRU_BLOCK_W
    nd, nb, k, _ = wa.shape
    eye = jnp.eye(per, dtype=wa.dtype)

    def bd(w):
        w = w.reshape(nd, nb // per, per, k, k)
        return jnp.einsum('dgpkj,pq->dgpkqj', w, eye).reshape(nd, nb // per, MXU_W, MXU_W)

    return jnp.concatenate([bd(wa), bd(wx)], axis=-1).astype(BF16)


def _merge_kernel(gb_ref, gc_ref, xc_ref, gcp_ref, xcp_ref, gcn_ref, xcn_ref,
                  lg_ref, hf_ref, hb_ref, ma_ref, mb_ref, x_ref,
                  cw_ref, woa_ref, wob_ref, wo_ref, bm_ref, gate_ref, nf_ref, sh_ref, sc_ref,
                  rwt_ref, rb_ref,
                  x1_ref, fx_ref, idx_ref, gts_ref):
    i = pl.program_id(1)
    n = pl.num_programs(1)
    tm, C = gc_ref.shape
    half = C // 2
    E = rwt_ref.shape[0]

    u = gc_ref[...].astype(F32) * xc_ref[...].astype(F32)
    uh = u[:, :half]
    col = lax.broadcasted_iota(jnp.int32, (tm, half), 0) % GRID_W
    left = jnp.where(col >= 1, pltpu.roll(uh, 1, 0), 0.0)
    right = jnp.where(col <= GRID_W - 2, pltpu.roll(uh, tm - 1, 0), 0.0)
    yh = (cw_ref[0:1, :half] * left + cw_ref[1:2, :half] * uh + cw_ref[2:3, :half] * right)
    uv = u[:, half:]
    prev = gcp_ref[...].astype(F32) * xcp_ref[...].astype(F32)
    nxt = gcn_ref[...].astype(F32) * xcn_ref[...].astype(F32)
    prev = jnp.where(i > 0, prev, 0.0)
    nxt = jnp.where(i < n - 1, nxt, 0.0)
    up = jnp.concatenate([prev, uv[:tm - GRID_W]], axis=0)
    down = jnp.concatenate([uv[GRID_W:], nxt], axis=0)
    yv = (cw_ref[0:1, half:] * up + cw_ref[1:2, half:] * uv + cw_ref[2:3, half:] * down)

    gb = gb_ref[...].astype(F32)
    za_h = (gb[:, :half] * yh).astype(BF16)
    za_v = (gb[:, half:] * yv).astype(BF16)
    y_a = (jnp.dot(za_h, woa_ref[:half, :], preferred_element_type=F32)
           + jnp.dot(za_v, woa_ref[half:, :], preferred_element_type=F32))

    h_lru = hf_ref[...].astype(F32) + hb_ref[...].astype(F32)
    zb = (jax.nn.gelu(lg_ref[...].astype(F32), approximate=True) * h_lru).astype(BF16)
    y_b = jnp.dot(zb, wob_ref[...], preferred_element_type=F32)

    merged = (jax.nn.sigmoid(ma_ref[...].astype(F32) + bm_ref[0:1, :]) * y_a
              + jax.nn.sigmoid(mb_ref[...].astype(F32) + bm_ref[1:2, :]) * y_b)
    mix = jnp.dot(merged.astype(BF16), wo_ref[...], preferred_element_type=F32)
    x1 = x_ref[...] + gate_ref[...] * mix
    x1_ref[...] = x1

    ms = jnp.mean(x1 * x1, axis=-1, keepdims=True)
    fx = (x1 * lax.rsqrt(ms + EPS) * nf_ref[...]) * (1.0 + sc_ref[...]) + sh_ref[...]
    fx_ref[...] = fx

    logits = lax.dot_general(rwt_ref[...], fx, (((1,), (1,)), ((), ())),
                             preferred_element_type=F32,
                             precision=lax.Precision.HIGHEST) + rb_ref[...]
    erow = lax.broadcasted_iota(jnp.int32, (E, tm), 0).astype(F32)
    vals = []
    for k in range(TOP_K):
        m = jnp.max(logits, axis=0, keepdims=True)
        am = jnp.min(jnp.where(logits == m, erow, float(E)), axis=0, keepdims=True)
        vals.append(m)
        idx_ref[k:k + 1, :] = am.astype(jnp.int32)
        logits = jnp.where(erow == am, -jnp.inf, logits)
    ex = [jnp.exp(v - vals[0]) for v in vals]
    den = ex[0] + ex[1] + ex[2] + ex[3]
    for k in range(TOP_K):
        gts_ref[k:k + 1, :] = ex[k] / den


def merge(cols, hf, hb, x, conv_w, w_out_a, w_out_b, w_o, b_merge, gate, norm_ffn, shift, scale,
          router_wt, router_b):
    gb, gc, xc, _, lg, ma, mb = cols
    B, S, D = x.shape
    C = gc.shape[-1]
    half = C // 2
    E = router_wt.shape[0]
    tm = min(TOKEN_TILE, S)
    n = S // tm
    r = tm // GRID_W
    n_rows = S // GRID_W
    tile = lambda w: pl.BlockSpec((None, tm, w), lambda b, i: (b, i, 0))
    prev = pl.BlockSpec((None, GRID_W, half), lambda b, i: (b, jnp.maximum(i * r - 1, 0), 1))
    nxt = pl.BlockSpec((None, GRID_W, half),
                       lambda b, i: (b, jnp.minimum((i + 1) * r, n_rows - 1), 1))
    vec = pl.BlockSpec((None, 1, D), lambda b, i: (b, 0, 0))
    lane_out = pl.BlockSpec((None, TOP_K, tm), lambda b, i: (b, 0, i))
    return pl.pallas_call(
        _merge_kernel,
        grid=(B, n),
        in_specs=[tile(C), tile(C), tile(C), prev, prev, nxt, nxt,
                  tile(C), tile(C), tile(C), tile(D), tile(D), tile(D),
                  _full(conv_w.shape), _full(w_out_a.shape), _full(w_out_b.shape),
                  _full(w_o.shape), _full(b_merge.shape), vec, _full((1, D)), vec, vec,
                  _full(router_wt.shape), _full((E, 1))],
        out_specs=[tile(D), tile(D), lane_out, lane_out],
        out_shape=[jax.ShapeDtypeStruct((B, S, D), F32), jax.ShapeDtypeStruct((B, S, D), F32),
                   jax.ShapeDtypeStruct((B, TOP_K, S), jnp.int32),
                   jax.ShapeDtypeStruct((B, TOP_K, S), F32)],
        compiler_params=_cparams(("arbitrary", "arbitrary")),
        name="merge",
    )(gb, gc, xc, gc, xc, gc, xc, lg, hf, hb, ma, mb, x,
      conv_w, w_out_a, w_out_b, w_o, b_merge, gate, norm_ffn.reshape(1, D), shift, scale,
      router_wt, router_b.reshape(E, 1))


def _selection(idx_ref, n_experts):
    tp = idx_ref.shape[1]
    idx = idx_ref[...]
    erow = lax.broadcasted_iota(jnp.int32, (n_experts, tp), 0)
    sel = [erow == idx[k:k + 1, :] for k in range(TOP_K)]
    onehot = sum(s.astype(F32) for s in sel)
    tile_cnt = jnp.sum(onehot, axis=1, keepdims=True).astype(jnp.int32)
    return sel, onehot, tile_cnt


def _round_up(v, m):
    s = m.bit_length() - 1
    return ((v + (m - 1)) >> s) << s


def _excl_cumsum(col):
    E = col.shape[0]
    r_i = lax.broadcasted_iota(jnp.int32, (E, E), 0)
    c_i = lax.broadcasted_iota(jnp.int32, (E, E), 1)
    row = jnp.sum(jnp.where(r_i == c_i, col, 0).astype(F32), axis=0, keepdims=True)
    return jnp.sum(jnp.where(c_i < r_i, row, 0.0), axis=1, keepdims=True).astype(jnp.int32)


def _tile_index():
    return pl.program_id(0) * pl.num_programs(1) + pl.program_id(1)


def _count_kernel(idx_ref, cp_ref):
    i = _tile_index()
    _, _, tile_cnt = _selection(idx_ref, cp_ref.shape[0])

    @pl.when(i == 0)
    def _():
        cp_ref[...] = jnp.zeros_like(cp_ref)

    lane = lax.broadcasted_iota(jnp.int32, cp_ref.shape, 1)
    cp_ref[...] += jnp.where(lane == i, _round_up(tile_cnt, ROW_ALIGN), 0)


def _plan_kernel(idx_ref, cp_ref, lpos_ref, tab_ref, blk_ref, base_s, tri_s):
    i = _tile_index()
    tp = idx_ref.shape[1]
    E = cp_ref.shape[0]
    n_blk = blk_ref.shape[1]
    sel, onehot, tile_cnt = _selection(idx_ref, E)
    run = _round_up(tile_cnt, ROW_ALIGN)

    @pl.when(i == 0)
    def _():
        used = jnp.sum(cp_ref[...].astype(F32), axis=1, keepdims=True).astype(jnp.int32)
        region = _round_up(used, MOE_BLOCK)
        pstart = _excl_cumsum(region)
        base_s[...] = pstart
        bs = lax.broadcasted_iota(jnp.int32, (E, n_blk), 1) * MOE_BLOCK
        inb = (bs >= pstart) & (bs < pstart + region)
        e_i = lax.broadcasted_iota(jnp.int32, (E, n_blk), 0)
        valid = jnp.clip(used - (bs - pstart), 0, MOE_BLOCK)
        blk_ref[0:1, :] = jnp.sum(jnp.where(inb, e_i, 0).astype(F32), axis=0,
                                  keepdims=True).astype(jnp.int32)
        blk_ref[1:2, :] = jnp.sum(jnp.where(inb, valid, 0).astype(F32), axis=0,
                                  keepdims=True).astype(jnp.int32)
        t_r = lax.broadcasted_iota(jnp.int32, (tp, tp), 0)
        t_c = lax.broadcasted_iota(jnp.int32, (tp, tp), 1)
        tri_s[...] = (t_r < t_c).astype(BF16)
        tab_ref[...] = jnp.zeros_like(tab_ref)

    loff = _excl_cumsum(run)
    pref = jnp.dot(onehot.astype(BF16), tri_s[...], preferred_element_type=F32)
    pos = pref + loff.astype(F32)
    for k in range(TOP_K):
        lpos_ref[k:k + 1, :] = jnp.sum(jnp.where(sel[k], pos, 0.0), axis=0,
                                       keepdims=True).astype(jnp.int32)

    hit = lax.broadcasted_iota(jnp.int32, tab_ref.shape[1:], 1) == i
    tab_ref[0] += jnp.where(hit, run, 0)
    tab_ref[1] += jnp.where(hit, loff, 0)
    tab_ref[2] += jnp.where(hit, base_s[...], 0)
    base_s[...] += run


def plan(idx, n_experts, n_blk):
    B, K, S = idx.shape
    tp = min(TOKEN_TILE, S)
    n_tiles = B * (S // tp)
    ntp = -(-n_tiles // LANES) * LANES
    lane = pl.BlockSpec((None, K, tp), lambda b, i: (b, 0, i))
    cp = pl.pallas_call(
        _count_kernel,
        grid=(B, S // tp),
        in_specs=[lane],
        out_specs=_full((n_experts, ntp)),
        out_shape=jax.ShapeDtypeStruct((n_experts, ntp), jnp.int32),
        compiler_params=_cparams(("arbitrary", "arbitrary")),
        name="count",
    )(idx)
    return pl.pallas_call(
        _plan_kernel,
        grid=(B, S // tp),
        in_specs=[lane, _full((n_experts, ntp))],
        out_specs=[lane, _full((3, n_experts, ntp)), _full((2, n_blk))],
        out_shape=[jax.ShapeDtypeStruct((B, K, S), jnp.int32),
                   jax.ShapeDtypeStruct((3, n_experts, ntp), jnp.int32),
                   jax.ShapeDtypeStruct((2, n_blk), jnp.int32)],
        scratch_shapes=[pltpu.VMEM((n_experts, 1), jnp.int32), pltpu.VMEM((tp, tp), BF16)],
        compiler_params=_cparams(("arbitrary", "arbitrary")),
        name="plan",
    )(idx, cp)


def _sorted_rows(tq, n_experts):
    return _round_up(TOP_K * tq + n_experts * (ROW_ALIGN - 1), ROW_ALIGN)


def _for_each_run_piece(tab_ref, i, fn):
    def per_expert(e, carry):
        run = tab_ref[0, e, i]
        done = jnp.int32(0)
        for size in RUN_SIZES:
            hit = (run & size) != 0

            @pl.when(hit)
            def _():
                fn(pl.multiple_of(tab_ref[1, e, i] + done, ROW_ALIGN),
                   pl.multiple_of(tab_ref[2, e, i] + done, ROW_ALIGN), size)

            done = done + jnp.where(hit, size, 0)
        return carry

    lax.fori_loop(0, tab_ref.shape[1], per_expert, 0)


def _for_each_gap_piece(blk_ref, fn):
    def per_block(j, carry):
        used = blk_ref[1, j]
        gap = MOE_BLOCK - used
        done = jnp.int32(0)
        for size in RUN_SIZES:
            hit = (gap & size) != 0

            @pl.when(hit)
            def _():
                fn(pl.multiple_of(j * MOE_BLOCK + used + done, ROW_ALIGN), size)

            done = done + jnp.where(hit, size, 0)
        return carry

    lax.fori_loop(0, blk_ref.shape[1], per_block, 0)


def _dispatch_kernel(tab_ref, blk_ref, lpos_ref, fx_ref, xs_hbm, sbuf, zbuf, sems):
    i = pl.program_id(0)
    last = pl.num_programs(0) - 1
    slot = i % 2
    rb, tq = sbuf.shape[1], fx_ref.shape[0]
    lpos = lpos_ref[...]
    r = lax.broadcasted_iota(jnp.int32, (rb, tq), 0)
    hit = r == lpos[0:1, :]
    for k in range(1, TOP_K):
        hit = hit | (r == lpos[k:k + 1, :])
    perm = jnp.where(hit, 1.0, 0.0).astype(BF16)
    sbuf[slot] = jnp.dot(perm, fx_ref[...].astype(BF16),
                         preferred_element_type=F32).astype(sbuf.dtype)

    def copy(s):
        return lambda lo, off, size: pltpu.make_async_copy(
            sbuf.at[s, pl.ds(lo, size), :], xs_hbm.at[pl.ds(off, size), :], sems.at[s])

    _for_each_run_piece(tab_ref, i, lambda *p: copy(slot)(*p).start())

    @pl.when(i > 0)
    def _():
        _for_each_run_piece(tab_ref, i - 1, lambda *p: copy(1 - slot)(*p).wait())

    @pl.when(i == last)
    def _():
        _for_each_run_piece(tab_ref, i, lambda *p: copy(slot)(*p).wait())
        zbuf[...] = jnp.zeros_like(zbuf)

        def fill(off, size):
            return pltpu.make_async_copy(zbuf.at[pl.ds(0, size), :],
                                         xs_hbm.at[pl.ds(off, size), :], sems.at[2])

        _for_each_gap_piece(blk_ref, lambda off, size: fill(off, size).start())
        _for_each_gap_piece(blk_ref, lambda off, size: fill(off, size).wait())


def dispatch(tab, blk, lpos, fx):
    B, K, S = lpos.shape
    D = fx.shape[-1]
    E = tab.shape[1]
    tq = min(TOKEN_TILE, S)
    n = S // tq
    n_rows = blk.shape[1] * MOE_BLOCK
    return pl.pallas_call(
        _dispatch_kernel,
        grid_spec=pltpu.PrefetchScalarGridSpec(
            num_scalar_prefetch=2,
            grid=(B * n,),
            in_specs=[pl.BlockSpec((None, K, tq), lambda i, tab, blk: (i // n, 0, i % n)),
                      pl.BlockSpec((tq, D), lambda i, tab, blk: (i, 0))],
            out_specs=pl.BlockSpec(memory_space=pl.ANY),
            scratch_shapes=[pltpu.VMEM((2, _sorted_rows(tq, E), D), F32),
                            pltpu.VMEM((MOE_BLOCK, D), F32),
                            pltpu.SemaphoreType.DMA((3,))]),
        out_shape=jax.ShapeDtypeStruct((n_rows, D), F32),
        compiler_params=_cparams(("arbitrary",)),
        name="dispatch",
    )(tab, blk, lpos, fx.reshape(B * S, D))


def _w1prep_kernel(w_ref, g_ref, l_ref):
    half = MXU_W // 2
    r = lax.broadcasted_iota(jnp.int32, (MXU_W, MXU_W), 0)
    c = lax.broadcasted_iota(jnp.int32, (MXU_W, MXU_W), 1)
    perm = (r == jnp.where(c < half, 2 * c, 2 * (c - half) + 1)).astype(BF16)
    for q in range(w_ref.shape[1] // MXU_W):
        blk = w_ref[:, q * MXU_W:(q + 1) * MXU_W].astype(BF16)
        out = jnp.dot(blk, perm, preferred_element_type=F32)
        g_ref[:, q * half:(q + 1) * half] = out[:, :half].astype(BF16)
        l_ref[:, q * half:(q + 1) * half] = out[:, half:].astype(BF16)


def w1prep(w1):
    E, D, F2 = w1.shape
    td = min(TOKEN_TILE, D)
    out = pl.BlockSpec((None, td, F2 // 2), lambda e, i: (e, i, 0))
    return pl.pallas_call(
        _w1prep_kernel,
        grid=(E, D // td),
        in_specs=[pl.BlockSpec((None, td, F2), lambda e, i: (e, i, 0))],
        out_specs=[out, out],
        out_shape=[jax.ShapeDtypeStruct((E, D, F2 // 2), BF16)] * 2,
        compiler_params=_cparams(("arbitrary", "arbitrary")),
        name="w1prep",
    )(w1)


def _expert_kernel(blk_ref, x_ref, w1g_ref, w1l_ref, b1g_ref, b1l_ref, w2_ref, b2_ref, y_ref):
    j = pl.program_id(0)
    nvalid = blk_ref[1, j]

    @pl.when(nvalid > 0)
    def _():
        x = x_ref[...].astype(BF16)
        hg = jnp.dot(x, w1g_ref[...], preferred_element_type=F32) + b1g_ref[...]
        hl = jnp.dot(x, w1l_ref[...], preferred_element_type=F32) + b1l_ref[...]
        glu = jnp.minimum(hg, SWIGLU_LIMIT)
        lin = jnp.clip(hl, -SWIGLU_LIMIT, SWIGLU_LIMIT)
        act = glu * jax.nn.sigmoid(SWIGLU_ALPHA * glu) * (lin + 1.0)
        y_ref[...] = jnp.dot(act.astype(BF16), w2_ref[...], preferred_element_type=F32) + b2_ref[...]

    @pl.when(nvalid == 0)
    def _():
        y_ref[...] = jnp.zeros_like(y_ref)


def experts(blk, xs, w1g, w1l, b1g, b1l, w2, b2):
    n_rows, D = xs.shape
    E, _, F = w1g.shape
    n_blk = n_rows // MOE_BLOCK
    rows = pl.BlockSpec((MOE_BLOCK, D), lambda j, blk: (j, 0))
    wsel = lambda a, b: pl.BlockSpec((None, a, b), lambda j, blk: (blk[0, j], 0, 0))
    return pl.pallas_call(
        _expert_kernel,
        grid_spec=pltpu.PrefetchScalarGridSpec(
            num_scalar_prefetch=1,
            grid=(n_blk,),
            in_specs=[rows, wsel(D, F), wsel(D, F), wsel(1, F), wsel(1, F), wsel(F, D), wsel(1, D)],
            out_specs=rows),
        out_shape=jax.ShapeDtypeStruct((n_rows, D), F32),
        compiler_params=_cparams(("arbitrary",)),
        name="experts",
    )(blk, xs, w1g, w1l, b1g, b1l, w2, b2)


def _combine_kernel(tab_ref, lpos_ref, gts_ref, ys_hbm, x1_ref, gate_ref, nf_ref, o_ref, ybuf, sems):
    i = pl.program_id(0)
    last = pl.num_programs(0) - 1
    slot = i % 2
    tq = x1_ref.shape[0]
    rb = ybuf.shape[1]

    def copy(s):
        return lambda lo, off, size: pltpu.make_async_copy(
            ys_hbm.at[pl.ds(off, size), :], ybuf.at[s, pl.ds(lo, size), :], sems.at[s])

    @pl.when(i == 0)
    def _():
        ybuf[...] = jnp.zeros_like(ybuf)
        _for_each_run_piece(tab_ref, i, lambda *p: copy(slot)(*p).start())

    @pl.when(i < last)
    def _():
        _for_each_run_piece(tab_ref, i + 1, lambda *p: copy(1 - slot)(*p).start())

    _for_each_run_piece(tab_ref, i, lambda *p: copy(slot)(*p).wait())

    lg = jnp.concatenate([lpos_ref[...].astype(F32), gts_ref[...]], axis=0).T
    c = lax.broadcasted_iota(jnp.int32, (tq, rb), 1).astype(F32)
    w = jnp.zeros((tq, rb), F32)
    for k in range(TOP_K):
        w = jnp.where(c == lg[:, k:k + 1], lg[:, TOP_K + k:TOP_K + k + 1], w)
    y = jnp.dot(w.astype(BF16), ybuf[slot].astype(BF16), preferred_element_type=F32)
    xo = x1_ref[...] + gate_ref[...] * y
    ms = jnp.mean(xo * xo, axis=-1, keepdims=True)
    o_ref[...] = xo * lax.rsqrt(ms + EPS) * nf_ref[...]


def combine(tab, lpos, gts, ys, x1, gate, norm_final):
    B, S, D = x1.shape
    K = lpos.shape[1]
    E = tab.shape[1]
    tq = min(TOKEN_TILE, S)
    n = S // tq
    lane = pl.BlockSpec((None, K, tq), lambda i, tab: (i // n, 0, i % n))
    tile = pl.BlockSpec((tq, D), lambda i, tab: (i, 0))
    out = pl.pallas_call(
        _combine_kernel,
        grid_spec=pltpu.PrefetchScalarGridSpec(
            num_scalar_prefetch=1,
            grid=(B * n,),
            in_specs=[lane, lane, pl.BlockSpec(memory_space=pl.ANY), tile,
                      pl.BlockSpec((None, 1, D), lambda i, tab: (i // n, 0, 0)),
                      pl.BlockSpec((1, D), lambda i, tab: (0, 0))],
            out_specs=tile,
            scratch_shapes=[pltpu.VMEM((2, _sorted_rows(tq, E), D), F32),
                            pltpu.SemaphoreType.DMA((2,))]),
        out_shape=jax.ShapeDtypeStruct((B * S, D), F32),
        compiler_params=_cparams(("arbitrary",)),
        name="combine",
    )(tab, lpos, gts, ys, x1.reshape(B * S, D), gate, norm_final.reshape(1, D))
    return out.reshape(B, S, D)


def kernel(x, c, ctx, c_ctx, w_ada, b_ada, norm_mix, w_in, conv_a_w, w_out_a, lru_conv_w,
           lru_conv_b, lru_wa, lru_ba, lru_wx, lru_bx, lru_lambda, w_out_b, b_merge, w_o,
           norm_ffn, router_w, router_b, w1, b1, w2, b2, norm_final):
    B, S, D = x.shape
    depth = w_ada.shape[0]
    assert depth == 1, "single-layer block"
    l = 0
    E = router_w.shape[-1]
    n_groups = w_in.shape[-1] // D
    lru_col = 3

    pad = -(B + 1) % SUBLANES
    cond = jnp.concatenate([c, c_ctx[None], jnp.zeros((pad, D), F32)], axis=0)
    mod = adaln(cond, w_ada[l], b_ada[l])
    mx = [mod[:B, i * D:(i + 1) * D].reshape(B, 1, D) for i in range(N_MOD)]
    mc = [jnp.broadcast_to(mod[B:B + 1, i * D:(i + 1) * D].reshape(1, 1, D), (B, 1, D))
          for i in range(2)]

    w_in_b = w_in[l].astype(BF16)
    wg = _gate_weights(lru_wa[l], lru_wx[l])
    lru_args = (lru_conv_w[l], lru_conv_b[l], wg, lru_ba[l], lru_bx[l], lru_lambda[l])

    (cs_lru,) = inproj(ctx, norm_mix[l], mc[0], mc[1], w_in_b[:, lru_col * D:(lru_col + 1) * D], 1)
    h_zero = jnp.zeros((2, B, 1, D), F32)
    _, _, h_ctx = lru(cs_lru, *lru_args, h_zero, BF16)

    cols = inproj(x, norm_mix[l], mx[0], mx[1], w_in_b, n_groups)
    hf, hb, _ = lru(cols[lru_col], *lru_args, h_ctx, BF16)

    x1, fx, idx, gts = merge(cols, hf, hb, x, conv_a_w[l], w_out_a[l].astype(BF16),
                             w_out_b[l].astype(BF16), w_o[l].astype(BF16), b_merge[l], mx[2],
                             norm_ffn[l], mx[3], mx[4], router_w[l].T, router_b[l])

    T = B * S
    n_tiles = T // min(TOKEN_TILE, S)
    n_blk = -(-(T * TOP_K + n_tiles * E * (ROW_ALIGN - 1)) // MOE_BLOCK) + E
    lpos, tab, blk = plan(idx, E, n_blk)
    xs = dispatch(tab, blk, lpos, fx)
    w1g, w1l = w1prep(w1[l])
    ys = experts(blk, xs, w1g, w1l,
                 b1[l][:, None, 0::2], b1[l][:, None, 1::2], w2[l].astype(BF16), b2[l][:, None, :])
    return combine(tab, lpos, gts, ys, x1, mx[5], norm_final)
```
